```python
import jax, jax.numpy as jnp
from jax import lax
import numpy as np

D_MODEL = 1024
BATCH = 4
SEQ = 8192
DEPTH = 2
DEC_BATCH = 32
DEC_SEQ = 1
PAST_LEN = 16384
PAGE_SIZE = 128

N_META = 16
D_MIX = D_MODEL
D_GLA = D_MIX // 2
D_FOX = D_MIX - D_GLA
GLA_HEADS = 4
GLA_DV = D_GLA // GLA_HEADS
GLA_DK = GLA_DV // 2
GLA_QK = GLA_HEADS * GLA_DK
GLA_RANK = 16
GLA_GATE_NORM = 16.0
GLA_CHUNK = 64
FOX_HEADS = 8
FOX_DH = D_FOX // FOX_HEADS
FOX_BLOCK = 128
FORGET_BIAS = 3.0
CACHE_FORGET_BIAS = 7.0
D_FF = 2816
D_IN = 2 * GLA_QK + 2 * D_GLA + GLA_RANK + 3 * D_FOX + FOX_HEADS
RMS_EPS = 1e-6

kernel_name = 'hymba_gla_fox_macaron_step'


def rmsnorm(x, w):
    xf = x.astype(jnp.float32)
    y = xf * lax.rsqrt(jnp.mean(xf * xf, axis=-1, keepdims=True) + RMS_EPS)
    return (y * w.astype(jnp.float32)).astype(x.dtype)


def swiglu(x, w_up, w_down):
    g, u = jnp.split(x @ w_up, 2, axis=-1)
    return (jax.nn.silu(g) * u) @ w_down


def project(u, w_in, b_f, w_a2, b_a2):
    sizes = (GLA_QK, GLA_QK, D_GLA, D_GLA, GLA_RANK, D_FOX, D_FOX, D_FOX, FOX_HEADS)
    idx, run = [], 0
    for s in sizes[:-1]:
        run += s
        idx.append(run)
    gq, gk, gv, gg, ga, fq, fk, fv, ff = jnp.split(u @ w_in, idx, axis=-1)
    B, T = u.shape[:2]
    f32 = jnp.float32
    gq = gq.astype(f32).reshape(B, T, GLA_HEADS, GLA_DK) * GLA_DK ** -0.5
    gk = gk.astype(f32).reshape(B, T, GLA_HEADS, GLA_DK)
    gv = gv.astype(f32).reshape(B, T, GLA_HEADS, GLA_DV)
    glog = jax.nn.log_sigmoid((ga @ w_a2 + b_a2).astype(f32)).reshape(B, T, GLA_HEADS, GLA_DK) / GLA_GATE_NORM
    fq = fq.reshape(B, T, FOX_HEADS, FOX_DH)
    fk = fk.reshape(B, T, FOX_HEADS, FOX_DH)
    fv = fv.reshape(B, T, FOX_HEADS, FOX_DH)
    flogf = jax.nn.log_sigmoid((ff + b_f).astype(f32))
    return gq, gk, gv, gg, glog, fq, fk, fv, flogf


def gla_chunk(S, q, k, v, lg):
    C = q.shape[1]
    G = jnp.cumsum(lg, axis=1)
    o_inter = jnp.einsum('bthk,bhkv->bthv', q * jnp.exp(G), S)
    causal = jnp.tril(jnp.ones((C, C), dtype=bool))[None, :, :, None, None]
    decay = jnp.exp(jnp.where(causal, G[:, :, None] - G[:, None, :], -jnp.inf))
    scores = jnp.einsum('bthk,bshk,btshk->bhts', q, k, decay)
    o_intra = jnp.einsum('bhts,bshv->bthv', scores, v)
    G_last = G[:, -1]
    S_new = jnp.exp(G_last)[..., None] * S + jnp.einsum('bshk,bshv->bhkv', k * jnp.exp(G_last[:, None] - G), v)
    return S_new, o_inter + o_intra


def gla_prompt(q, k, v, lg):
    B = q.shape[0]
    S0 = jnp.zeros((B, GLA_HEADS, GLA_DK, GLA_DV), jnp.float32)
    S1, o_meta = gla_chunk(S0, q[:, :N_META], k[:, :N_META], v[:, :N_META], lg[:, :N_META])

    def to_chunks(a):
        a = a[:, N_META:]
        nc = a.shape[1] // GLA_CHUNK
        return jnp.moveaxis(a.reshape(B, nc, GLA_CHUNK, *a.shape[2:]), 1, 0)

    def step(S, xs):
        return gla_chunk(S, *xs)

    S_fin, o_real = lax.scan(step, S1, (to_chunks(q), to_chunks(k), to_chunks(v), to_chunks(lg)))
    o_real = jnp.moveaxis(o_real, 0, 1).reshape(B, -1, GLA_HEADS, GLA_DV)
    return S_fin, jnp.concatenate([o_meta, o_real], axis=1)


def fox_block(qb, cq, kb, vb, ck, q0):
    nq, nk = qb.shape[1], kb.shape[1]
    s = jnp.einsum('bthd,bshd->bhts', qb.astype(jnp.float32), kb.astype(jnp.float32)) * FOX_DH ** -0.5
    s = s + jnp.moveaxis(cq, 2, 1)[..., :, None] - jnp.moveaxis(ck, 2, 1)[..., None, :]
    mask = (q0 + jnp.arange(nq))[:, None] >= jnp.arange(nk)[None, :]
    p = jax.nn.softmax(jnp.where(mask, s, -jnp.inf), axis=-1)
    return jnp.einsum('bhts,bshd->bthd', p, vb.astype(jnp.float32))


def fox_prompt(q, k, v, logf):
    L = q.shape[1]
    c = jnp.cumsum(logf, axis=1)
    bounds = [(0, N_META)] + [(N_META + i * FOX_BLOCK, N_META + (i + 1) * FOX_BLOCK)
                              for i in range((L - N_META) // FOX_BLOCK)]
    outs = [fox_block(q[:, q0:q1], c[:, q0:q1], k[:, :q1], v[:, :q1], c[:, :q1], q0) for q0, q1 in bounds]
    return jnp.concatenate(outs, axis=1)


def fox_sample(q, k, v, logf, ck_pool, cv_pool, clf_pool, page_table):
    DB, T = q.shape[:2]
    f32 = jnp.float32
    pk = ck_pool[page_table].reshape(DB, -1, FOX_HEADS, FOX_DH)
    pv = cv_pool[page_table].reshape(DB, -1, FOX_HEADS, FOX_DH)
    plf = clf_pool[page_table].reshape(DB, -1, FOX_HEADS).astype(f32)
    P = pk.shape[1]
    suffix = lax.cumsum(plf, axis=1, reverse=True) - plf
    ct = jnp.moveaxis(jnp.cumsum(logf, axis=1), 2, 1)
    qf = q.astype(f32) * FOX_DH ** -0.5
    s_past = jnp.einsum('bthd,bshd->bhts', qf, pk.astype(f32)) + ct[..., None] + jnp.moveaxis(suffix, 2, 1)[:, :, None, :]
    s_new = jnp.einsum('bthd,bshd->bhts', qf, k.astype(f32)) + ct[..., None] - ct[:, :, None, :]
    mask = jnp.tril(jnp.ones((T, T), dtype=bool))
    s_new = jnp.where(mask, s_new, -jnp.inf)
    p = jax.nn.softmax(jnp.concatenate([s_past, s_new], axis=-1), axis=-1)
    return (jnp.einsum('bhts,bshd->bthd', p[..., :P], pv.astype(f32))
            + jnp.einsum('bhts,bshd->bthd', p[..., P:], v.astype(f32)))


def merge(o_gla, g_out, o_fox, norm_w, w_out):
    B, T = o_gla.shape[:2]
    on = o_gla * lax.rsqrt(jnp.mean(o_gla * o_gla, axis=-1, keepdims=True) + RMS_EPS) * norm_w.astype(jnp.float32)
    gla_part = on.reshape(B, T, D_GLA) * jax.nn.silu(g_out.astype(jnp.float32))
    mixed = jnp.concatenate([gla_part, o_fox.reshape(B, T, D_FOX)], axis=-1).astype(w_out.dtype)
    return mixed @ w_out


def setup_inputs(seed: int = 0) -> dict:
    key = jax.random.key(seed)
    ks = jax.random.split(key, 24)
    n_pages = PAST_LEN // PAGE_SIZE
    n_used = DEC_BATCH * n_pages
    n_pool = n_used + max(1, n_used // 4)
    nrm = jax.random.normal
    page_table = jax.random.permutation(ks[0], n_pool)[:n_used].reshape(DEC_BATCH, n_pages).astype(jnp.int32)
    return {
        'x_prompt': nrm(ks[1], (BATCH, SEQ, D_MODEL), jnp.float32),
        'x_sample': nrm(ks[2], (DEC_BATCH, DEC_SEQ, D_MODEL), jnp.float32),
        'cache_fox_k': nrm(ks[3], (DEPTH, n_pool, PAGE_SIZE, FOX_HEADS, FOX_DH), jnp.float32),
        'cache_fox_v': nrm(ks[4], (DEPTH, n_pool, PAGE_SIZE, FOX_HEADS, FOX_DH), jnp.float32),
        'cache_fox_logf': jax.nn.log_sigmoid(CACHE_FORGET_BIAS + 0.5 * nrm(ks[5], (DEPTH, n_pool, PAGE_SIZE, FOX_HEADS), jnp.float32)),
        'state_gla': nrm(ks[6], (DEPTH, DEC_BATCH, GLA_HEADS, GLA_DK, GLA_DV), jnp.float32),
        'page_table': page_table,
        'meta_tokens': nrm(ks[7], (N_META, D_MODEL), jnp.float32),
        'ffn1_norm': 1.0 + 0.02 * nrm(ks[8], (DEPTH, D_MODEL), jnp.float32),
        'ffn1_w_up': nrm(ks[9], (DEPTH, D_MODEL, 2 * D_FF), jnp.float32) * D_MODEL ** -0.5,
        'ffn1_w_down': nrm(ks[10], (DEPTH, D_FF, D_MODEL), jnp.float32) * D_FF ** -0.5,
        'mix_norm': 1.0 + 0.02 * nrm(ks[11], (DEPTH, D_MODEL), jnp.float32),
        'w_in': nrm(ks[12], (DEPTH, D_MODEL, D_IN), jnp.float32) * D_MODEL ** -0.5,
        'fox_b_f': FORGET_BIAS + 0.1 * nrm(ks[13], (DEPTH, FOX_HEADS), jnp.float32),
        'gla_w_a2': nrm(ks[14], (DEPTH, GLA_RANK, GLA_QK), jnp.float32) * GLA_RANK ** -0.5,
        'gla_b_a2': 0.1 * nrm(ks[15], (DEPTH, GLA_QK), jnp.float32),
        'gla_norm': 1.0 + 0.02 * nrm(ks[16], (DEPTH, GLA_DV), jnp.float32),
        'w_out': nrm(ks[17], (DEPTH, D_MIX, D_MODEL), jnp.float32) * D_MIX ** -0.5,
        'ffn2_norm': 1.0 + 0.02 * nrm(ks[18], (DEPTH, D_MODEL), jnp.float32),
        'ffn2_w_up': nrm(ks[19], (DEPTH, D_MODEL, 2 * D_FF), jnp.float32) * D_MODEL ** -0.5,
        'ffn2_w_down': nrm(ks[20], (DEPTH, D_FF, D_MODEL), jnp.float32) * D_FF ** -0.5,
        'final_norm': 1.0 + 0.02 * nrm(ks[21], (D_MODEL,), jnp.float32),
    }


def reference(x_prompt, x_sample, cache_fox_k, cache_fox_v, cache_fox_logf, state_gla, page_table,
              meta_tokens, ffn1_norm, ffn1_w_up, ffn1_w_down, mix_norm, w_in, fox_b_f, gla_w_a2,
              gla_b_a2, gla_norm, w_out, ffn2_norm, ffn2_w_up, ffn2_w_down, final_norm):
    B = x_prompt.shape[0]
    meta = jnp.broadcast_to(meta_tokens.astype(x_prompt.dtype)[None], (B, N_META, D_MODEL))
    hp = jnp.concatenate([meta, x_prompt], axis=1)
    hs = x_sample
    pk, pv, plf, pS, sk, sv, slf, sS = [], [], [], [], [], [], [], []
    for l in range(DEPTH):
        hp = hp + 0.5 * swiglu(rmsnorm(hp, ffn1_norm[l]), ffn1_w_up[l], ffn1_w_down[l])
        hs = hs + 0.5 * swiglu(rmsnorm(hs, ffn1_norm[l]), ffn1_w_up[l], ffn1_w_down[l])
        gq, gk, gv, gg, glg, fq, fk, fv, flf = project(rmsnorm(hp, mix_norm[l]), w_in[l], fox_b_f[l], gla_w_a2[l], gla_b_a2[l])
        S_p, o_gla = gla_prompt(gq, gk, gv, glg)
        o_fox = fox_prompt(fq, fk, fv, flf)
        hp = hp + merge(o_gla, gg, o_fox, gla_norm[l], w_out[l]).astype(hp.dtype)
        pk.append(fk); pv.append(fv); plf.append(flf); pS.append(S_p)

        sq_, sk_, sv_, sg_, slg_, sfq, sfk, sfv, sflf = project(rmsnorm(hs, mix_norm[l]), w_in[l], fox_b_f[l], gla_w_a2[l], gla_b_a2[l])
        S_s, o_gla_s = gla_chunk(state_gla[l].astype(jnp.float32), sq_, sk_, sv_, slg_)
        o_fox_s = fox_sample(sfq, sfk, sfv, sflf, cache_fox_k[l], cache_fox_v[l], cache_fox_logf[l], page_table)
        hs = hs + merge(o_gla_s, sg_, o_fox_s, gla_norm[l], w_out[l]).astype(hs.dtype)
        sk.append(sfk); sv.append(sfv); slf.append(sflf); sS.append(S_s)
        hp = hp + 0.5 * swiglu(rmsnorm(hp, ffn2_norm[l]), ffn2_w_up[l], ffn2_w_down[l])
        hs = hs + 0.5 * swiglu(rmsnorm(hs, ffn2_norm[l]), ffn2_w_up[l], ffn2_w_down[l])
    y_prompt = rmsnorm(hp, final_norm)[:, N_META:]
    y_sample = rmsnorm(hs, final_norm)
    return (y_prompt, y_sample, jnp.stack(pk), jnp.stack(pv), jnp.stack(plf), jnp.stack(pS),
            jnp.stack(sk), jnp.stack(sv), jnp.stack(slf), jnp.stack(sS))
```

```python
import functools

import jax
import jax.numpy as jnp
from jax import lax
from jax.experimental import pallas as pl
from jax.experimental.pallas import tpu as pltpu

F32 = jnp.float32
BF16 = jnp.bfloat16

LANES = 128
SIDE_ROWS = 128
RMS_EPS = 1e-6
NEG = -1e30
GLA_HEADS, GLA_DK, GLA_DV, GLA_RANK = 4, 64, 128, 16
GLA_GATE_NORM = 16.0
FOX_HEADS, FOX_DH = 8, 64
D_GLA = GLA_HEADS * GLA_DV
D_FOX = FOX_HEADS * FOX_DH
GLA_QK = GLA_HEADS * GLA_DK
VMEM_LIMIT = 56 * 1024 * 1024

NT_DIMS = (((1,), (1,)), ((), ()))
TN_DIMS = (((0,), (0,)), ((), ()))


def _cparams(sem):
    return pltpu.CompilerParams(dimension_semantics=sem, vmem_limit_bytes=VMEM_LIMIT)


def _resident(shape):
    nd = len(shape)
    return pl.BlockSpec(shape, lambda *_: (0,) * nd, pipeline_mode=pl.Buffered(1))


def _rms(x, w):
    return x * lax.rsqrt(jnp.mean(x * x, axis=-1, keepdims=True) + RMS_EPS) * w


def _logsig(x):
    return jnp.minimum(x, 0.0) - jnp.log1p(jnp.exp(-jnp.abs(x)))


def _split3(x):
    hi = x.astype(BF16)
    r1 = x - hi.astype(F32)
    mid = r1.astype(BF16)
    lo = (r1 - mid.astype(F32)).astype(BF16)
    return hi, mid, lo


def _dot01_right(x, m01):
    hi, mid, lo = _split3(x)
    d = lambda a: jnp.dot(a, m01, preferred_element_type=F32)
    return (d(lo) + d(mid)) + d(hi)


def _dot01_left(m01, x):
    hi, mid, lo = _split3(x)
    d = lambda a: jnp.dot(m01, a, preferred_element_type=F32)
    return (d(lo) + d(mid)) + d(hi)


def _ffn_body(h_ref, nw_ref, wup_ref, wdn_ref, o_ref, *, ff, ck):
    x = h_ref[...]
    xn = _rms(x, nw_ref[...]).astype(BF16)
    acc = jnp.zeros(x.shape, F32)
    for c in range(ff // ck):
        g = jnp.dot(xn, wup_ref[:, c * ck:(c + 1) * ck], preferred_element_type=F32)
        u = jnp.dot(xn, wup_ref[:, ff + c * ck:ff + (c + 1) * ck], preferred_element_type=F32)
        hm = (g * jax.nn.sigmoid(g) * u).astype(BF16)
        acc = acc + jnp.dot(hm, wdn_ref[c * ck:(c + 1) * ck, :], preferred_element_type=F32)
    o_ref[...] = x + 0.5 * acc


def _ffn(h, nw, wup, wdn, tm):
    rows, d = h.shape
    ff = wdn.shape[0]
    ck = 256 if ff % 256 == 0 else LANES
    return pl.pallas_call(
        functools.partial(_ffn_body, ff=ff, ck=ck),
        grid=(rows // tm,),
        in_specs=[pl.BlockSpec((tm, d), lambda i: (i, 0)),
                  _resident((1, d)), _resident(wup.shape), _resident(wdn.shape)],
        out_specs=pl.BlockSpec((tm, d), lambda i: (i, 0)),
        out_shape=jax.ShapeDtypeStruct((rows, d), F32),
        compiler_params=_cparams(("parallel",)),
        name="ffn",
    )(h, nw, wup, wdn)


def _proj_body(h_ref, nw_ref, wmain_ref, wsmall_ref, wa2_ref, ba2_ref, bsm_ref, u_ref,
               gq_ref, gk_ref, gv_ref, gg_ref, glog_ref, fq_ref, fkb_ref, fvb_ref,
               fk_ref, fv_ref, lf_ref, ckt_ref, carry_ref, *, tiles_per_seq):
    @pl.when(pl.program_id(0) % tiles_per_seq == 0)
    def _():
        carry_ref[...] = jnp.zeros_like(carry_ref)

    x = h_ref[...]
    tm = x.shape[0]
    xn = _rms(x, nw_ref[...]).astype(BF16)

    def seg(i):
        return jnp.dot(xn, wmain_ref[:, i * 512:(i + 1) * 512], preferred_element_type=F32)

    z = seg(0)
    gq_ref[...] = z[:, :GLA_QK] * (GLA_DK ** -0.5)
    gk_ref[...] = z[:, GLA_QK:]
    gv_ref[...] = seg(1).astype(BF16)
    gg_ref[...] = seg(2)
    fq_ref[...] = (seg(3) * (FOX_DH ** -0.5)).astype(BF16)
    z = seg(4)
    fk_ref[...] = z
    fkb_ref[...] = z.astype(BF16)
    z = seg(5)
    fv_ref[...] = z
    fvb_ref[...] = z.astype(BF16)

    zs = jnp.dot(xn, wsmall_ref[...], preferred_element_type=F32)
    lfull = _logsig(zs + bsm_ref[...])
    lf_ref[...] = lfull[:, :FOX_HEADS]
    x2 = jnp.dot(zs.astype(BF16), wa2_ref[...], preferred_element_type=F32) + ba2_ref[...]
    glog_ref[...] = _logsig(x2) * (1.0 / GLA_GATE_NORM)

    lt = lfull.T[:FOX_HEADS, :]
    carry = carry_ref[:, 0:1]
    for j in range(tm // LANES):
        c = _dot01_right(lt[:, j * LANES:(j + 1) * LANES], u_ref[...]) + carry
        ckt_ref[:, j * LANES:(j + 1) * LANES] = c
        carry = c[:, LANES - 1:LANES]
    carry_ref[...] = jnp.broadcast_to(carry, carry_ref.shape)


def _proj(h, nw, wmain, wsmall, wa2, ba2, bsm, tm, rows_per_seq):
    rows, d = h.shape
    upper = (lax.broadcasted_iota(jnp.int32, (LANES, LANES), 0)
             <= lax.broadcasted_iota(jnp.int32, (LANES, LANES), 1)).astype(BF16)
    row_spec = lambda w: pl.BlockSpec((tm, w), lambda i: (i, 0))
    sds = lambda w, dt: jax.ShapeDtypeStruct((rows, w), dt)
    outs = pl.pallas_call(
        functools.partial(_proj_body, tiles_per_seq=rows_per_seq // tm),
        grid=(rows // tm,),
        in_specs=[row_spec(d), _resident((1, d)), _resident(wmain.shape), _resident(wsmall.shape),
                  _resident(wa2.shape), _resident(ba2.shape), _resident(bsm.shape), _resident(upper.shape)],
        out_specs=[row_spec(GLA_QK), row_spec(GLA_QK), row_spec(D_GLA), row_spec(D_GLA), row_spec(GLA_QK),
                   row_spec(D_FOX), row_spec(D_FOX), row_spec(D_FOX), row_spec(D_FOX), row_spec(D_FOX),
                   row_spec(FOX_HEADS), pl.BlockSpec((FOX_HEADS, tm), lambda i: (0, i))],
        out_shape=[sds(GLA_QK, F32), sds(GLA_QK, F32), sds(D_GLA, BF16), sds(D_GLA, F32), sds(GLA_QK, F32),
                   sds(D_FOX, BF16), sds(D_FOX, BF16), sds(D_FOX, BF16), sds(D_FOX, F32), sds(D_FOX, F32),
                   sds(FOX_HEADS, F32), jax.ShapeDtypeStruct((FOX_HEADS, rows), F32)],
        scratch_shapes=[pltpu.VMEM((FOX_HEADS, LANES), F32)],
        compiler_params=_cparams(("arbitrary",)),
        name="proj",
    )(h, nw, wmain, wsmall, wa2, ba2, bsm, upper)
    keys = ("gq", "gk", "gv", "gg", "glog", "fq", "fkb", "fvb", "fk", "fv", "lf", "ckt")
    return dict(zip(keys, outs))


def _gla_body(q_ref, k_ref, v_ref, lg_ref, s0_ref, o_ref, st_ref, st_scr, *, n_valid):
    c = pl.program_id(2)

    @pl.when(c == 0)
    def _():
        st_scr[...] = s0_ref[...]

    q = q_ref[...]
    k = k_ref[...]
    lg = lg_ref[...]
    v = v_ref[...]
    cs = q.shape[0]
    row = lax.broadcasted_iota(jnp.int32, (cs, cs), 0)
    col = lax.broadcasted_iota(jnp.int32, (cs, cs), 1)
    tri = row >= col
    if n_valid is not None:
        valid = lax.broadcasted_iota(jnp.int32, (cs, 1), 0) < n_valid
        k = jnp.where(valid, k, 0.0)
        lg = jnp.where(valid, lg, 0.0)

    g = _dot01_left(tri.astype(BF16), lg)
    g_last = g[cs - 1:cs, :]
    g_mid = g[cs // 2 - 1:cs // 2, :]
    qg = (q * jnp.exp(g - g_mid)).astype(BF16)
    kg = (k * jnp.exp(g_mid - g)).astype(BF16)
    kd = (k * jnp.exp(g_last - g)).astype(BF16)
    qi = (q * jnp.exp(g)).astype(BF16)

    st = st_scr[...]
    o_inter = lax.dot_general(qi, st.astype(BF16), NT_DIMS, preferred_element_type=F32)
    lane = lax.broadcasted_iota(jnp.int32, (1, LANES), 1)
    for hh in range(2):
        head = (lane < GLA_DK) if hh == 0 else (lane >= GLA_DK)
        a = lax.dot_general(jnp.where(head, qg, jnp.zeros_like(qg)), kg, NT_DIMS,
                            preferred_element_type=F32)
        a = jnp.where(tri, a, 0.0).astype(BF16)
        o_intra = jnp.dot(a, v[:, hh * GLA_DV:(hh + 1) * GLA_DV], preferred_element_type=F32)
        o_ref[:, hh * GLA_DV:(hh + 1) * GLA_DV] = o_inter[:, hh * GLA_DV:(hh + 1) * GLA_DV] + o_intra

    upd = lax.dot_general(v, kd, TN_DIMS, preferred_element_type=F32)
    vrow = lax.broadcasted_iota(jnp.int32, upd.shape, 0) < GLA_DV
    klane = lax.broadcasted_iota(jnp.int32, upd.shape, 1) < GLA_DK
    st_new = st * jnp.exp(g_last) + jnp.where(vrow == klane, upd, 0.0)
    st_scr[...] = st_new

    @pl.when(c == pl.num_programs(2) - 1)
    def _():
        st_ref[...] = st_new


def _gla(q, k, v, lg, s0t, cs, n_valid=None):
    bg, t, _ = q.shape
    qk_spec = pl.BlockSpec((None, cs, LANES), lambda b, p, c: (b, c, p))
    v_spec = pl.BlockSpec((None, cs, 2 * GLA_DV), lambda b, p, c: (b, c, p))
    if s0t.shape[0] == 1:
        s0_spec = pl.BlockSpec((None, None, 2 * GLA_DV, LANES), lambda b, p, c: (0, p, 0, 0))
    else:
        s0_spec = pl.BlockSpec((None, None, 2 * GLA_DV, LANES), lambda b, p, c: (b, p, 0, 0))
    return pl.pallas_call(
        functools.partial(_gla_body, n_valid=n_valid),
        grid=(bg, 2, t // cs),
        in_specs=[qk_spec, qk_spec, v_spec, qk_spec, s0_spec],
        out_specs=[v_spec, pl.BlockSpec((None, None, 2 * GLA_DV, LANES), lambda b, p, c: (b, p, 0, 0))],
        out_shape=[jax.ShapeDtypeStruct((bg, t, D_GLA), F32),
                   jax.ShapeDtypeStruct((bg, 2, 2 * GLA_DV, LANES), F32)],
        scratch_shapes=[pltpu.VMEM((2 * GLA_DV, LANES), F32)],
        compiler_params=_cparams(("parallel", "parallel", "arbitrary")),
        name="gla",
    )(q, k, v, lg, s0t)


def _state_to_pairs(s):
    bs = s.shape[0]
    st = jnp.swapaxes(s, -1, -2).reshape(bs, 2, 2, GLA_DV, GLA_DK)
    z = jnp.zeros_like(st[:, :, 0])
    top = jnp.concatenate([st[:, :, 0], z], axis=-1)
    bot = jnp.concatenate([z, st[:, :, 1]], axis=-1)
    return jnp.concatenate([top, bot], axis=-2)


def _pairs_to_state(st):
    a = st[:, :, :GLA_DV, :GLA_DK]
    b = st[:, :, GLA_DV:, GLA_DK:]
    s = jnp.stack([a, b], axis=2)
    return jnp.swapaxes(s, -1, -2).reshape(st.shape[0], GLA_HEADS, GLA_DK, GLA_DV)


def _fox_body(*refs, has_prefix):
    if has_prefix:
        q_ref, k_ref, v_ref, ck_ref, km_ref, vm_ref, bm_ref, o_ref, m_scr, l_scr, acc_scr = refs
    else:
        q_ref, k_ref, v_ref, ck_ref, o_ref, m_scr, l_scr, acc_scr = refs
    pair = pl.program_id(1)
    i = pl.program_id(2)
    q = q_ref[...]
    tq = q.shape[0]
    lane = lax.broadcasted_iota(jnp.int32, (1, LANES), 1)
    first = lane < FOX_DH
    qh = (jnp.where(first, q, jnp.zeros_like(q)), jnp.where(first, jnp.zeros_like(q), q))

    m_scr[...] = jnp.full(m_scr.shape, NEG, F32)
    l_scr[...] = jnp.zeros(l_scr.shape, F32)
    acc_scr[...] = jnp.zeros(acc_scr.shape, F32)

    def tile(kb, vb, bias, mask):
        for hh in range(2):
            s = lax.dot_general(qh[hh], kb, NT_DIMS, preferred_element_type=F32) + bias[hh]
            if mask is not None:
                s = jnp.where(mask, s, NEG)
            m_old = m_scr[hh]
            m_new = jnp.maximum(m_old, jnp.max(s, axis=-1, keepdims=True))
            p = jnp.exp(s - m_new)
            alpha = jnp.exp(m_old - m_new)
            l_scr[hh] = alpha * l_scr[hh] + jnp.sum(p, axis=-1, keepdims=True)
            acc_scr[hh] = alpha * acc_scr[hh] + jnp.dot(p.astype(BF16), vb, preferred_element_type=F32)
            m_scr[hh] = m_new

    def key_bias(j):
        return [-ck_ref[2 * pair + hh, pl.ds(j, 1), :] for hh in range(2)]

    if has_prefix:
        tile(km_ref[...], vm_ref[...], [bm_ref[pl.ds(2 * pair + hh, 1), :] for hh in range(2)], None)

    def body(j, carry):
        off = pl.multiple_of(j * tq, tq)
        tile(k_ref[pl.ds(off, tq), :], v_ref[pl.ds(off, tq), :], key_bias(j), None)
        return carry

    lax.fori_loop(0, i, body, 0)
    off = pl.multiple_of(i * tq, tq)
    causal = (lax.broadcasted_iota(jnp.int32, (tq, tq), 0) >= lax.broadcasted_iota(jnp.int32, (tq, tq), 1))
    tile(k_ref[pl.ds(off, tq), :], v_ref[pl.ds(off, tq), :], key_bias(i), causal)

    o0 = acc_scr[0] / l_scr[0]
    o1 = acc_scr[1] / l_scr[1]
    o_ref[...] = jnp.where(first, o0, o1).astype(BF16)


def _fox(q, k, v, ck, tq, prefix=None):
    b, l, _ = q.shape
    nq = l // tq
    q_spec = pl.BlockSpec((None, tq, LANES), lambda bb, p, i: (bb, i, p))
    kv_spec = pl.BlockSpec((None, l, LANES), lambda bb, p, i: (bb, 0, p))
    ck_spec = pl.BlockSpec((None, FOX_HEADS, nq, tq), lambda bb, p, i: (bb, 0, 0, 0))
    in_specs = [q_spec, kv_spec, kv_spec, ck_spec]
    args = [q, k, v, ck]
    if prefix is not None:
        pk_spec = pl.BlockSpec((SIDE_ROWS, LANES), lambda bb, p, i: (0, p))
        in_specs += [pk_spec, pk_spec, pl.BlockSpec((FOX_HEADS, SIDE_ROWS), lambda bb, p, i: (0, 0))]
        args += list(prefix)
    return pl.pallas_call(
        functools.partial(_fox_body, has_prefix=prefix is not None),
        grid=(b, FOX_HEADS // 2, nq),
        in_specs=in_specs,
        out_specs=q_spec,
        out_shape=jax.ShapeDtypeStruct((b, l, D_FOX), BF16),
        scratch_shapes=[pltpu.VMEM((2, tq, 1), F32), pltpu.VMEM((2, tq, 1), F32),
                        pltpu.VMEM((2, tq, LANES), F32)],
        compiler_params=_cparams(("parallel", "parallel", "arbitrary")),
        name="fox",
    )(*args)


def _decode_body(pt_ref, qbd_ref, knew_ref, vnew_ref, lfnew_ref, u_ref, *rest, n_pages):
    del pt_ref
    k_refs = rest[:n_pages]
    v_refs = rest[n_pages:2 * n_pages]
    lf_refs = rest[2 * n_pages:3 * n_pages]
    o_ref, m_scr, l_scr, acc_scr, carry_scr = rest[3 * n_pages:]
    j = pl.program_id(1)
    qbd = qbd_ref[...]
    width = qbd.shape[1]
    head_lane = (lax.broadcasted_iota(jnp.int32, (FOX_HEADS, width), 1) // FOX_DH
                 == lax.broadcasted_iota(jnp.int32, (FOX_HEADS, width), 0))

    @pl.when(j == 0)
    def _():
        s_new = jnp.sum(qbd.astype(F32) * knew_ref[...], axis=-1, keepdims=True)
        m_scr[...] = jnp.broadcast_to(s_new, m_scr.shape)
        l_scr[...] = jnp.ones(l_scr.shape, F32)
        acc_scr[...] = jnp.where(head_lane, jnp.broadcast_to(vnew_ref[...], acc_scr.shape), 0.0)
        carry_scr[...] = jnp.zeros(carry_scr.shape, F32)

    lf_new = lfnew_ref[...]
    for g in reversed(range(n_pages)):
        lft = lf_refs[g][...]
        suffix = _dot01_right(lft, u_ref[...])
        carry = carry_scr[...]
        s = lax.dot_general(qbd, k_refs[g][...].astype(BF16), NT_DIMS, preferred_element_type=F32)
        s = s + (suffix + carry + lf_new)
        m_old = m_scr[...]
        m_new = jnp.maximum(m_old, jnp.max(s, axis=-1, keepdims=True))
        p = jnp.exp(s - m_new)
        alpha = jnp.exp(m_old - m_new)
        l_scr[...] = alpha * l_scr[...] + jnp.sum(p, axis=-1, keepdims=True)
        acc_scr[...] = alpha[:, 0:1] * acc_scr[...] + jnp.dot(
            p.astype(BF16), v_refs[g][...].astype(BF16), preferred_element_type=F32)
        m_scr[...] = m_new
        carry_scr[...] = carry + jnp.sum(lft, axis=-1, keepdims=True)

    @pl.when(j == pl.num_programs(1) - 1)
    def _():
        o = jnp.where(head_lane, acc_scr[...] / l_scr[:, 0:1], 0.0)
        o_ref[...] = jnp.sum(o, axis=0, keepdims=True)


def _decode(layer, page_table, qbd, knew, vnew, lfnew, cache_k, cache_v, cache_lft, n_pages):
    db, total_pages = page_table.shape
    page = cache_k.shape[2]
    width = cache_k.shape[3]
    groups = total_pages // n_pages
    strict_upper = (lax.broadcasted_iota(jnp.int32, (page, page), 0)
                    > lax.broadcasted_iota(jnp.int32, (page, page), 1)).astype(BF16)

    def page_map(g):
        return lambda b, j, pt: (layer, pt[b, (groups - 1 - j) * n_pages + g], 0, 0)

    seq3 = lambda b, j, pt: (b, 0, 0)
    kv_specs = [pl.BlockSpec((None, None, page, width), page_map(g)) for g in range(n_pages)]
    lf_specs = [pl.BlockSpec((None, None, FOX_HEADS, page), page_map(g)) for g in range(n_pages)]
    grid_spec = pltpu.PrefetchScalarGridSpec(
        num_scalar_prefetch=1,
        grid=(db, groups),
        in_specs=[pl.BlockSpec((None, FOX_HEADS, width), seq3), pl.BlockSpec((None, 1, width), seq3),
                  pl.BlockSpec((None, 1, width), seq3), pl.BlockSpec((None, FOX_HEADS, 1), seq3),
                  pl.BlockSpec((page, page), lambda b, j, pt: (0, 0))] + kv_specs + kv_specs + lf_specs,
        out_specs=pl.BlockSpec((None, 1, width), seq3),
        scratch_shapes=[pltpu.VMEM((FOX_HEADS, LANES), F32), pltpu.VMEM((FOX_HEADS, LANES), F32),
                        pltpu.VMEM((FOX_HEADS, width), F32), pltpu.VMEM((FOX_HEADS, LANES), F32)],
    )
    return pl.pallas_call(
        functools.partial(_decode_body, n_pages=n_pages),
        grid_spec=grid_spec,
        out_shape=jax.ShapeDtypeStruct((db, 1, width), F32),
        compiler_params=_cparams(("parallel", "arbitrary")),
        name="fox_decode",
    )(page_table, qbd, knew, vnew, lfnew, strict_upper,
      *([cache_k] * n_pages), *([cache_v] * n_pages), *([cache_lft] * n_pages))


def _merge_body(h_ref, og_ref, gg_ref, of_ref, gn_ref, wo_ref, o_ref):
    og = og_ref[...]
    gg = gg_ref[...]
    gn = gn_ref[...]
    parts = []
    for hd in range(GLA_HEADS):
        oh = og[:, hd * GLA_DV:(hd + 1) * GLA_DV]
        on = oh * lax.rsqrt(jnp.mean(oh * oh, axis=-1, keepdims=True) + RMS_EPS) * gn
        gh = gg[:, hd * GLA_DV:(hd + 1) * GLA_DV]
        parts.append((on * (gh * jax.nn.sigmoid(gh))).astype(BF16))
    gla_part = jnp.concatenate(parts, axis=-1)
    mixed = (jnp.dot(gla_part, wo_ref[:D_GLA, :], preferred_element_type=F32)
             + jnp.dot(of_ref[...], wo_ref[D_GLA:, :], preferred_element_type=F32))
    o_ref[...] = h_ref[...] + mixed


def _merge(h, o_gla, gg, o_fox, gn, wo, tm):
    rows, d = h.shape
    row_spec = lambda w: pl.BlockSpec((tm, w), lambda i: (i, 0))
    return pl.pallas_call(
        _merge_body,
        grid=(rows // tm,),
        in_specs=[row_spec(d), row_spec(D_GLA), row_spec(D_GLA), row_spec(D_FOX),
                  _resident(gn.shape), _resident(wo.shape)],
        out_specs=row_spec(d),
        out_shape=jax.ShapeDtypeStruct((rows, d), F32),
        compiler_params=_cparams(("parallel",)),
        name="merge",
    )(h, o_gla, gg, o_fox, gn, wo)


def _norm_body(h_ref, nw_ref, o_ref):
    o_ref[...] = _rms(h_ref[...], nw_ref[...])


def _norm(h, nw, tm):
    rows, d = h.shape
    return pl.pallas_call(
        _norm_body,
        grid=(rows // tm,),
        in_specs=[pl.BlockSpec((tm, d), lambda i: (i, 0)), _resident((1, d))],
        out_specs=pl.BlockSpec((tm, d), lambda i: (i, 0)),
        out_shape=jax.ShapeDtypeStruct((rows, d), F32),
        compiler_params=_cparams(("parallel",)),
        name="final_norm",
    )(h, nw)


def _pick(n, prefs):
    for t in prefs:
        if n % t == 0:
            return t
    raise ValueError(f"no tile in {prefs} divides {n}")


def _pack_w_in(w_in_l, b_f_l, w_a2_l, b_a2_l):
    sizes = (GLA_QK, GLA_QK, D_GLA, D_GLA, GLA_RANK, D_FOX, D_FOX, D_FOX, FOX_HEADS)
    offs = [0]
    for s in sizes:
        offs.append(offs[-1] + s)
    col = lambda i: w_in_l[:, offs[i]:offs[i + 1]]
    wmain = jnp.concatenate([col(0), col(1), col(2), col(3), col(5), col(6), col(7)], axis=1).astype(BF16)
    d = w_in_l.shape[0]
    pad = LANES - FOX_HEADS - GLA_RANK
    wsmall = jnp.concatenate([col(8), col(4), jnp.zeros((d, pad), F32)], axis=1).astype(BF16)
    bsm = jnp.concatenate([b_f_l, jnp.zeros((LANES - FOX_HEADS,), F32)])[None, :]
    wa2 = jnp.concatenate([jnp.zeros((FOX_HEADS, GLA_QK), F32), w_a2_l, jnp.zeros((pad, GLA_QK), F32)],
                          axis=0).astype(BF16)
    return wmain, wsmall, wa2, b_a2_l[None, :], bsm


def kernel(x_prompt, x_sample, cache_fox_k, cache_fox_v, cache_fox_logf, state_gla, page_table, meta_tokens,
           ffn1_norm, ffn1_w_up, ffn1_w_down, mix_norm, w_in, fox_b_f, gla_w_a2, gla_b_a2, gla_norm, w_out,
           ffn2_norm, ffn2_w_up, ffn2_w_down, final_norm):
    b, seq, d = x_prompt.shape
    db, dec_seq, _ = x_sample.shape
    n_meta = meta_tokens.shape[0]
    depth = w_in.shape[0]
    n_pool, page = cache_fox_k.shape[1], cache_fox_k.shape[2]
    assert dec_seq == 1 and n_meta + db <= SIDE_ROWS and page == LANES
    s0, s1 = n_meta, n_meta + db

    tm = _pick(seq, (512, 256, 128))
    tq = _pick(seq, (512, 256, 128))
    cs = _pick(seq, (64,))
    pages_per_step = _pick(page_table.shape[1], (8, 4, 2, 1))

    hm = x_prompt.reshape(b * seq, d)
    side = jnp.concatenate([meta_tokens, x_sample[:, 0, :], jnp.zeros((SIDE_ROWS - s1, d), F32)], axis=0)

    cache_k = cache_fox_k.reshape(depth, n_pool, page, D_FOX)
    cache_v = cache_fox_v.reshape(depth, n_pool, page, D_FOX)
    cache_lft = jnp.swapaxes(cache_fox_logf, -1, -2)

    outs = {n: [] for n in ("pk", "pv", "plf", "ps", "sk", "sv", "slf", "ss")}
    for l in range(depth):
        up1, dn1 = ffn1_w_up[l].astype(BF16), ffn1_w_down[l].astype(BF16)
        up2, dn2 = ffn2_w_up[l].astype(BF16), ffn2_w_down[l].astype(BF16)
        wmain, wsmall, wa2, ba2, bsm = _pack_w_in(w_in[l], fox_b_f[l], gla_w_a2[l], gla_b_a2[l])
        wo = w_out[l].astype(BF16)

        hm = _ffn(hm, ffn1_norm[l][None, :], up1, dn1, tm)
        side = _ffn(side, ffn1_norm[l][None, :], up1, dn1, SIDE_ROWS)
        pm = _proj(hm, mix_norm[l][None, :], wmain, wsmall, wa2, ba2, bsm, tm, seq)
        sd = _proj(side, mix_norm[l][None, :], wmain, wsmall, wa2, ba2, bsm, SIDE_ROWS, SIDE_ROWS)

        zero_state = jnp.zeros((1, 2, 2 * GLA_DV, LANES), F32)
        o_gla_meta, st_meta = _gla(sd["gq"][None], sd["gk"][None], sd["gv"][None], sd["glog"][None],
                                   zero_state, SIDE_ROWS, n_valid=n_meta)
        r3 = lambda a: a.reshape(b, seq, a.shape[-1])
        o_gla, st_prompt = _gla(r3(pm["gq"]), r3(pm["gk"]), r3(pm["gv"]), r3(pm["glog"]), st_meta, cs)
        pad_tok = lambda a: jnp.pad(a[s0:s1][:, None, :], ((0, 0), (0, SIDE_ROWS - 1), (0, 0)))
        o_gla_s, st_sample = _gla(pad_tok(sd["gq"]), pad_tok(sd["gk"]), pad_tok(sd["gv"]), pad_tok(sd["glog"]),
                                  _state_to_pairs(state_gla[l]), SIDE_ROWS, n_valid=1)

        o_fox_meta = _fox(sd["fq"][None], sd["fkb"][None], sd["fvb"][None],
                          sd["ckt"].reshape(1, FOX_HEADS, 1, SIDE_ROWS), SIDE_ROWS)
        c_meta = sd["ckt"]
        meta_lane = jnp.arange(SIDE_ROWS)[None, :] < n_meta
        bias_meta = jnp.where(meta_lane, c_meta[:, n_meta - 1:n_meta] - c_meta, NEG)
        ck_main = jnp.transpose(pm["ckt"].reshape(FOX_HEADS, b, seq // tq, tq), (1, 0, 2, 3))
        o_fox = _fox(r3(pm["fq"]), r3(pm["fkb"]), r3(pm["fvb"]), ck_main, tq,
                     prefix=(sd["fkb"], sd["fvb"], bias_meta))

        q_s = sd["fq"][s0:s1]
        head_of_lane = jnp.arange(D_FOX)[None, None, :] // FOX_DH
        qbd = jnp.where(head_of_lane == jnp.arange(FOX_HEADS)[None, :, None], q_s[:, None, :], 0).astype(BF16)
        o_fox_s = _decode(l, page_table, qbd, sd["fk"][s0:s1][:, None, :], sd["fv"][s0:s1][:, None, :],
                          sd["lf"][s0:s1][:, :, None], cache_k, cache_v, cache_lft, pages_per_step)

        tail = jnp.zeros((SIDE_ROWS - s1, D_GLA), F32)
        og_side = jnp.concatenate([o_gla_meta[0, :n_meta], o_gla_s[:, 0, :], tail], axis=0)
        of_side = jnp.concatenate([o_fox_meta[0, :n_meta], o_fox_s[:, 0, :].astype(BF16), tail.astype(BF16)], axis=0)
        gn = gla_norm[l][None, :]
        hm = _merge(hm, o_gla.reshape(b * seq, D_GLA), pm["gg"], o_fox.reshape(b * seq, D_FOX), gn, wo, tm)
        side = _merge(side, og_side, sd["gg"], of_side, gn, wo, SIDE_ROWS)
        hm = _ffn(hm, ffn2_norm[l][None, :], up2, dn2, tm)
        side = _ffn(side, ffn2_norm[l][None, :], up2, dn2, SIDE_ROWS)

        heads = lambda a: a.reshape(a.shape[:-1] + (FOX_HEADS, FOX_DH))
        with_meta = lambda m, r: jnp.concatenate(
            [jnp.broadcast_to(m[None, :n_meta], (b, n_meta) + m.shape[1:]), r.reshape((b, seq) + r.shape[1:])], axis=1)
        outs["pk"].append(heads(with_meta(sd["fk"], pm["fk"])))
        outs["pv"].append(heads(with_meta(sd["fv"], pm["fv"])))
        outs["plf"].append(with_meta(sd["lf"], pm["lf"]))
        outs["ps"].append(_pairs_to_state(st_prompt))
        outs["sk"].append(heads(sd["fk"][s0:s1][:, None, :]))
        outs["sv"].append(heads(sd["fv"][s0:s1][:, None, :]))
        outs["slf"].append(sd["lf"][s0:s1][:, None, :])
        outs["ss"].append(_pairs_to_state(st_sample))

    y_prompt = _norm(hm, final_norm[None, :], tm).reshape(b, seq, d)
    y_sample = _norm(side, final_norm[None, :], SIDE_ROWS)[s0:s1][:, None, :]
    st = lambda n: jnp.stack(outs[n])
    return (y_prompt, y_sample, st("pk"), st("pv"), st("plf"), st("ps"), st("sk"), st("sv"), st("slf"), st("ss"))
```

```python
import functools

import jax
import jax.numpy as jnp
from jax import lax
from jax.experimental import pallas as pl
from jax.experimental.pallas import tpu as pltpu

F32 = jnp.float32
BF16 = jnp.bfloat16

LANES = 128
SIDE_ROWS = 128
RMS_EPS = 1e-6
NEG = -1e30
GLA_HEADS, GLA_DK, GLA_DV, GLA_RANK = 4, 64, 128, 16
GLA_GATE_NORM = 16.0
FOX_HEADS, FOX_DH = 8, 64
D_GLA = GLA_HEADS * GLA_DV
D_FOX = FOX_HEADS * FOX_DH
GLA_QK = GLA_HEADS * GLA_DK
VMEM_LIMIT = 56 * 1024 * 1024

NT_DIMS = (((1,), (1,)), ((), ()))
TN_DIMS = (((0,), (0,)), ((), ()))


def _cparams(sem):
    return pltpu.CompilerParams(dimension_semantics=sem, vmem_limit_bytes=VMEM_LIMIT)


def _resident(shape):
    nd = len(shape)
    return pl.BlockSpec(shape, lambda *_: (0,) * nd, pipeline_mode=pl.Buffered(1))


def _rms(x, w):
    return x * lax.rsqrt(jnp.mean(x * x, axis=-1, keepdims=True) + RMS_EPS) * w


def _logsig(x):
    return jnp.minimum(x, 0.0) - jnp.log1p(jnp.exp(-jnp.abs(x)))


def _split3(x):
    hi = x.astype(BF16)
    r1 = x - hi.astype(F32)
    mid = r1.astype(BF16)
    lo = (r1 - mid.astype(F32)).astype(BF16)
    return hi, mid, lo


def _dot01_right(x, m01):
    hi, mid, lo = _split3(x)
    d = lambda a: jnp.dot(a, m01, preferred_element_type=F32)
    return (d(lo) + d(mid)) + d(hi)


def _dot01_left(m01, x):
    hi, mid, lo = _split3(x)
    d = lambda a: jnp.dot(m01, a, preferred_element_type=F32)
    return (d(lo) + d(mid)) + d(hi)


def _ffn_body(h_ref, nw_ref, wup_ref, wdn_ref, o_ref, *, ff, ck):
    x = h_ref[...]
    xn = _rms(x, nw_ref[...]).astype(BF16)
    acc = jnp.zeros(x.shape, F32)
    for c in range(ff // ck):
        g = jnp.dot(xn, wup_ref[:, c * ck:(c + 1) * ck], preferred_element_type=F32)
        u = jnp.dot(xn, wup_ref[:, ff + c * ck:ff + (c + 1) * ck], preferred_element_type=F32)
        hm = (g * jax.nn.sigmoid(g) * u).astype(BF16)
        acc = acc + jnp.dot(hm, wdn_ref[c * ck:(c + 1) * ck, :], preferred_element_type=F32)
    o_ref[...] = x + 0.5 * acc


def _ffn(h, nw, wup, wdn, tm):
    rows, d = h.shape
    ff = wdn.shape[0]
    ck = 256 if ff % 256 == 0 else LANES
    return pl.pallas_call(
        functools.partial(_ffn_body, ff=ff, ck=ck),
        grid=(rows // tm,),
        in_specs=[pl.BlockSpec((tm, d), lambda i: (i, 0)),
                  _resident((1, d)), _resident(wup.shape), _resident(wdn.shape)],
        out_specs=pl.BlockSpec((tm, d), lambda i: (i, 0)),
        out_shape=jax.ShapeDtypeStruct((rows, d), F32),
        compiler_params=_cparams(("parallel",)),
        name="ffn",
    )(h, nw, wup, wdn)


def _proj_body(h_ref, nw_ref, wmain_ref, wsmall_ref, wa2_ref, ba2_ref, bsm_ref, tri_ref, place_ref,
               gq_ref, gk_ref, gv_ref, gg_ref, glog_ref, qa_ref, ka_ref, va_ref,
               fk_ref, fv_ref, lf_ref, carry_ref, *, tiles_per_seq, prefix_rows):
    @pl.when(pl.program_id(0) % tiles_per_seq == 0)
    def _():
        carry_ref[...] = jnp.zeros_like(carry_ref)

    x = h_ref[...]
    tm = x.shape[0]
    xn = _rms(x, nw_ref[...]).astype(BF16)

    def seg(i):
        return jnp.dot(xn, wmain_ref[:, i * 512:(i + 1) * 512], preferred_element_type=F32)

    z = seg(0)
    gq_ref[...] = z[:, :GLA_QK] * (GLA_DK ** -0.5)
    gk_ref[...] = z[:, GLA_QK:]
    gv_ref[...] = seg(1).astype(BF16)
    gg_ref[...] = seg(2)

    zs = jnp.dot(xn, wsmall_ref[...], preferred_element_type=F32)
    lfull = _logsig(zs + bsm_ref[...])
    lf_ref[...] = lfull[:, :FOX_HEADS]
    x2 = jnp.dot(zs.astype(BF16), wa2_ref[...], preferred_element_type=F32) + ba2_ref[...]
    glog_ref[...] = _logsig(x2) * (1.0 / GLA_GATE_NORM)

    c = _dot01_left(tri_ref[...], lfull) + carry_ref[0:1, :]
    carry_ref[...] = jnp.broadcast_to(c[tm - 1:tm, :], carry_ref.shape)
    if prefix_rows is None:
        bias = -c
    else:
        rows = lax.broadcasted_iota(jnp.int32, (tm, 1), 0)
        bias = jnp.where(rows < prefix_rows, c[prefix_rows - 1:prefix_rows, :] - c, NEG)
    b_hi, b_mid, b_lo = _split3(bias)
    place = lambda part, i: jnp.dot(part, place_ref[i], preferred_element_type=F32)
    baug = place(b_hi, 0) + place(b_mid, 1) + place(b_lo, 2)

    lane = lax.broadcasted_iota(jnp.int32, (1, LANES), 1)
    data = lane < FOX_DH
    q_aug = jnp.where(lane < FOX_DH + 3, 1.0, 0.0)
    zq = seg(3) * (FOX_DH ** -0.5)
    zk = seg(4)
    zv = seg(5)
    fk_ref[...] = zk
    fv_ref[...] = zv
    for p in range(FOX_HEADS // 2):
        sl = slice(p * LANES, (p + 1) * LANES)
        for hh in range(2):
            h = 2 * p + hh
            hs = slice(h * LANES, (h + 1) * LANES)
            pick = (lambda t: t) if hh == 0 else (lambda t: pltpu.roll(t, FOX_DH, 1))
            qa_ref[:, hs] = jnp.where(data, pick(zq[:, sl]), q_aug).astype(BF16)
            ka_ref[:, hs] = jnp.where(data, pick(zk[:, sl]), baug[:, hs]).astype(BF16)
            va_ref[:, hs] = jnp.where(data, pick(zv[:, sl]), 1.0).astype(BF16)


def _proj(h, nw, wmain, wsmall, wa2, ba2, bsm, tm, rows_per_seq, prefix_rows=None):
    rows, d = h.shape
    tri = (lax.broadcasted_iota(jnp.int32, (tm, tm), 0)
           >= lax.broadcasted_iota(jnp.int32, (tm, tm), 1)).astype(BF16)
    src_row = lax.broadcasted_iota(jnp.int32, (3, LANES, FOX_HEADS * LANES), 1)
    dst = lax.broadcasted_iota(jnp.int32, (3, LANES, FOX_HEADS * LANES), 2)
    part = lax.broadcasted_iota(jnp.int32, (3, LANES, FOX_HEADS * LANES), 0)
    place = ((dst == src_row * LANES + FOX_DH + part) & (src_row < FOX_HEADS)).astype(BF16)
    row_spec = lambda w: pl.BlockSpec((tm, w), lambda i: (i, 0))
    sds = lambda w, dt: jax.ShapeDtypeStruct((rows, w), dt)
    aug = FOX_HEADS * LANES
    outs = pl.pallas_call(
        functools.partial(_proj_body, tiles_per_seq=rows_per_seq // tm, prefix_rows=prefix_rows),
        grid=(rows // tm,),
        in_specs=[row_spec(d), _resident((1, d)), _resident(wmain.shape), _resident(wsmall.shape),
                  _resident(wa2.shape), _resident(ba2.shape), _resident(bsm.shape), _resident(tri.shape),
                  _resident(place.shape)],
        out_specs=[row_spec(GLA_QK), row_spec(GLA_QK), row_spec(D_GLA), row_spec(D_GLA), row_spec(GLA_QK),
                   row_spec(aug), row_spec(aug), row_spec(aug), row_spec(D_FOX), row_spec(D_FOX),
                   row_spec(FOX_HEADS)],
        out_shape=[sds(GLA_QK, F32), sds(GLA_QK, F32), sds(D_GLA, BF16), sds(D_GLA, F32), sds(GLA_QK, F32),
                   sds(aug, BF16), sds(aug, BF16), sds(aug, BF16), sds(D_FOX, F32), sds(D_FOX, F32),
                   sds(FOX_HEADS, F32)],
        scratch_shapes=[pltpu.VMEM((8, LANES), F32)],
        compiler_params=_cparams(("arbitrary",)),
        name="proj",
    )(h, nw, wmain, wsmall, wa2, ba2, bsm, tri, place)
    keys = ("gq", "gk", "gv", "gg", "glog", "qa", "ka", "va", "fk", "fv", "lf")
    return dict(zip(keys, outs))


def _gla_body(q_ref, k_ref, v_ref, lg_ref, s0_ref, o_ref, st_ref, st_scr, *, n_valid):
    c = pl.program_id(2)

    @pl.when(c == 0)
    def _():
        st_scr[...] = s0_ref[...]

    q = q_ref[...]
    k = k_ref[...]
    lg = lg_ref[...]
    v = v_ref[...]
    cs = q.shape[0]
    row = lax.broadcasted_iota(jnp.int32, (cs, cs), 0)
    col = lax.broadcasted_iota(jnp.int32, (cs, cs), 1)
    tri = row >= col
    if n_valid is not None:
        valid = lax.broadcasted_iota(jnp.int32, (cs, 1), 0) < n_valid
        k = jnp.where(valid, k, 0.0)
        lg = jnp.where(valid, lg, 0.0)

    g = _dot01_left(tri.astype(BF16), lg)
    g_last = g[cs - 1:cs, :]
    g_mid = g[cs // 2 - 1:cs // 2, :]
    qg = (q * jnp.exp(g - g_mid)).astype(BF16)
    kg = (k * jnp.exp(g_mid - g)).astype(BF16)
    kd = (k * jnp.exp(g_last - g)).astype(BF16)
    qi = (q * jnp.exp(g)).astype(BF16)

    st = st_scr[...]
    o_inter = lax.dot_general(qi, st.astype(BF16), NT_DIMS, preferred_element_type=F32)
    lane = lax.broadcasted_iota(jnp.int32, (1, LANES), 1)
    for hh in range(2):
        head = (lane < GLA_DK) if hh == 0 else (lane >= GLA_DK)
        a = lax.dot_general(jnp.where(head, qg, jnp.zeros_like(qg)), kg, NT_DIMS,
                            preferred_element_type=F32)
        a = jnp.where(tri, a, 0.0).astype(BF16)
        o_intra = jnp.dot(a, v[:, hh * GLA_DV:(hh + 1) * GLA_DV], preferred_element_type=F32)
        o_ref[:, hh * GLA_DV:(hh + 1) * GLA_DV] = o_inter[:, hh * GLA_DV:(hh + 1) * GLA_DV] + o_intra

    upd = lax.dot_general(v, kd, TN_DIMS, preferred_element_type=F32)
    vrow = lax.broadcasted_iota(jnp.int32, upd.shape, 0) < GLA_DV
    klane = lax.broadcasted_iota(jnp.int32, upd.shape, 1) < GLA_DK
    st_new = st * jnp.exp(g_last) + jnp.where(vrow == klane, upd, 0.0)
    st_scr[...] = st_new

    @pl.when(c == pl.num_programs(2) - 1)
    def _():
        st_ref[...] = st_new


def _gla(q, k, v, lg, s0t, cs, n_valid=None):
    bg, t, _ = q.shape
    qk_spec = pl.BlockSpec((None, cs, LANES), lambda b, p, c: (b, c, p))
    v_spec = pl.BlockSpec((None, cs, 2 * GLA_DV), lambda b, p, c: (b, c, p))
    if s0t.shape[0] == 1:
        s0_spec = pl.BlockSpec((None, None, 2 * GLA_DV, LANES), lambda b, p, c: (0, p, 0, 0))
    else:
        s0_spec = pl.BlockSpec((None, None, 2 * GLA_DV, LANES), lambda b, p, c: (b, p, 0, 0))
    return pl.pallas_call(
        functools.partial(_gla_body, n_valid=n_valid),
        grid=(bg, 2, t // cs),
        in_specs=[qk_spec, qk_spec, v_spec, qk_spec, s0_spec],
        out_specs=[v_spec, pl.BlockSpec((None, None, 2 * GLA_DV, LANES), lambda b, p, c: (b, p, 0, 0))],
        out_shape=[jax.ShapeDtypeStruct((bg, t, D_GLA), F32),
                   jax.ShapeDtypeStruct((bg, 2, 2 * GLA_DV, LANES), F32)],
        scratch_shapes=[pltpu.VMEM((2 * GLA_DV, LANES), F32)],
        compiler_params=_cparams(("parallel", "parallel", "arbitrary")),
        name="gla",
    )(q, k, v, lg, s0t)


def _state_to_pairs(s):
    bs = s.shape[0]
    st = jnp.swapaxes(s, -1, -2).reshape(bs, 2, 2, GLA_DV, GLA_DK)
    z = jnp.zeros_like(st[:, :, 0])
    top = jnp.concatenate([st[:, :, 0], z], axis=-1)
    bot = jnp.concatenate([z, st[:, :, 1]], axis=-1)
    return jnp.concatenate([top, bot], axis=-2)


def _pairs_to_state(st):
    a = st[:, :, :GLA_DV, :GLA_DK]
    b = st[:, :, GLA_DV:, GLA_DK:]
    s = jnp.stack([a, b], axis=2)
    return jnp.swapaxes(s, -1, -2).reshape(st.shape[0], GLA_HEADS, GLA_DK, GLA_DV)


def _fox_body(*refs, has_prefix):
    if has_prefix:
        q_ref, k_ref, v_ref, km_ref, vm_ref, o_ref, m_scr, acc_scr = refs
    else:
        q_ref, k_ref, v_ref, o_ref, m_scr, acc_scr = refs
    i = pl.program_id(2)
    tq = q_ref.shape[0]
    m_scr[...] = jnp.full(m_scr.shape, NEG, F32)
    acc_scr[...] = jnp.zeros(acc_scr.shape, F32)

    def tile(kfn, vfn, mask):
        for hh in range(2):
            hs = slice(hh * LANES, (hh + 1) * LANES)
            s = lax.dot_general(q_ref[:, hs], kfn(hs), NT_DIMS, preferred_element_type=F32)
            if mask is not None:
                s = jnp.where(mask, s, NEG)
            nk = s.shape[1]
            mx = s[:, :LANES]
            for t in range(1, nk // LANES):
                mx = jnp.maximum(mx, s[:, t * LANES:(t + 1) * LANES])
            m_old = m_scr[hh]
            m_new = jnp.maximum(m_old, jnp.max(mx, axis=-1, keepdims=True))
            p = jnp.exp(s - pltpu.repeat(m_new, nk // LANES, axis=1))
            alpha = jnp.exp(m_old - m_new)
            acc_scr[hh] = alpha * acc_scr[hh] + jnp.dot(p.astype(BF16), vfn(hs), preferred_element_type=F32)
            m_scr[hh] = m_new

    if has_prefix:
        tile(lambda hs: km_ref[:, hs], lambda hs: vm_ref[:, hs], None)

    def body(j, carry):
        off = pl.multiple_of(j * tq, tq)
        tile(lambda hs: k_ref[pl.ds(off, tq), hs], lambda hs: v_ref[pl.ds(off, tq), hs], None)
        return carry

    lax.fori_loop(0, i, body, 0)
    off = pl.multiple_of(i * tq, tq)
    causal = (lax.broadcasted_iota(jnp.int32, (tq, tq), 0) >= lax.broadcasted_iota(jnp.int32, (tq, tq), 1))
    tile(lambda hs: k_ref[pl.ds(off, tq), hs], lambda hs: v_ref[pl.ds(off, tq), hs], causal)

    outs = []
    for hh in range(2):
        acc = acc_scr[hh]
        outs.append(acc / pltpu.roll(acc, FOX_DH, 1))
    first = lax.broadcasted_iota(jnp.int32, (1, LANES), 1) < FOX_DH
    o_ref[...] = jnp.where(first, outs[0], pltpu.roll(outs[1], FOX_DH, 1)).astype(BF16)


def _fox(qa, ka, va, tq, prefix=None):
    b, l, _ = qa.shape
    q_spec = pl.BlockSpec((None, tq, 2 * LANES), lambda bb, p, i: (bb, i, p))
    kv_spec = pl.BlockSpec((None, l, 2 * LANES), lambda bb, p, i: (bb, 0, p))
    in_specs = [q_spec, kv_spec, kv_spec]
    args = [qa, ka, va]
    if prefix is not None:
        pk_spec = pl.BlockSpec((SIDE_ROWS, 2 * LANES), lambda bb, p, i: (0, p))
        in_specs += [pk_spec, pk_spec]
        args += list(prefix)
    return pl.pallas_call(
        functools.partial(_fox_body, has_prefix=prefix is not None),
        grid=(b, FOX_HEADS // 2, l // tq),
        in_specs=in_specs,
        out_specs=pl.BlockSpec((None, tq, LANES), lambda bb, p, i: (bb, i, p)),
        out_shape=jax.ShapeDtypeStruct((b, l, D_FOX), BF16),
        scratch_shapes=[pltpu.VMEM((2, tq, LANES), F32), pltpu.VMEM((2, tq, LANES), F32)],
        compiler_params=_cparams(("parallel", "parallel", "arbitrary")),
        name="fox",
    )(*args)


def _decode_body(pt_ref, q_ref, knew_ref, vnew_ref, lfnew_ref, mw_ref, mt_ref, *rest, n_pages):
    del pt_ref
    k_refs = rest[:n_pages]
    v_refs = rest[n_pages:2 * n_pages]
    lf_refs = rest[2 * n_pages:3 * n_pages]
    o_ref, m_scr, l_scr, acc_scr, carry_scr = rest[3 * n_pages:]
    j = pl.program_id(1)
    q = q_ref[...]
    nblk = k_refs[0].shape[0] * FOX_HEADS // LANES
    sub = lax.broadcasted_iota(jnp.int32, (FOX_HEADS, LANES), 0)
    lane = lax.broadcasted_iota(jnp.int32, (FOX_HEADS, LANES), 1)
    own = (lane % FOX_HEADS) == sub

    @pl.when(j == 0)
    def _():
        s_new = jnp.sum(q * knew_ref[...], axis=-1, keepdims=True)
        m_scr[...] = jnp.broadcast_to(s_new, m_scr.shape)
        l_scr[...] = jnp.ones(l_scr.shape, F32)
        acc_scr[...] = vnew_ref[...]
        carry_scr[...] = jnp.zeros(carry_scr.shape, F32)

    qb = q.astype(BF16)
    carry = carry_scr[...] + lfnew_ref[...]
    blocks = []
    for g in reversed(range(n_pages)):
        lf = lf_refs[g][...]
        within = _dot01_right(lf, mw_ref[...])
        tot = _dot01_right(lf, mt_ref[...])
        suf = tot
        for step in (1, 2, 4):
            shifted = pltpu.roll(suf, FOX_HEADS - step, 0)
            suf = suf + jnp.where(sub + step < FOX_HEADS, shifted, 0.0)
        bias = within + (suf - tot) + carry
        carry = carry + suf[0:1, :]
        k2 = k_refs[g][...].reshape(-1, FOX_DH).astype(BF16)
        s = lax.dot_general(qb, k2, NT_DIMS, preferred_element_type=F32)
        blocks.append([jnp.where(own, s[:, r * LANES:(r + 1) * LANES] + bias[r:r + 1, :], NEG)
                       for r in range(nblk)])
    carry_scr[...] = carry - lfnew_ref[...]

    mx = blocks[0][0]
    for blk in sum(blocks, [])[1:]:
        mx = jnp.maximum(mx, blk)
    m_old = m_scr[...]
    m_new = jnp.maximum(m_old, jnp.max(mx, axis=-1, keepdims=True))
    alpha = jnp.exp(m_old - m_new)
    psum = jnp.zeros((FOX_HEADS, LANES), F32)
    pv = jnp.zeros((FOX_HEADS, FOX_DH), F32)
    for idx, g in enumerate(reversed(range(n_pages))):
        p = jnp.concatenate([jnp.exp(blk - m_new) for blk in blocks[idx]], axis=1)
        for r in range(nblk):
            psum = psum + p[:, r * LANES:(r + 1) * LANES]
        v2 = v_refs[g][...].reshape(-1, FOX_DH).astype(BF16)
        pv = pv + jnp.dot(p.astype(BF16), v2, preferred_element_type=F32)
    l_scr[...] = alpha * l_scr[...] + jnp.sum(psum, axis=-1, keepdims=True)
    acc_scr[...] = alpha[:, :FOX_DH] * acc_scr[...] + pv
    m_scr[...] = m_new

    @pl.when(j == pl.num_programs(1) - 1)
    def _():
        o_ref[...] = acc_scr[...] / l_scr[:, :FOX_DH]


def _decode(layer, page_table, q, knew, vnew, lfnew, cache_k, cache_v, cache_lf, n_pages):
    db, total_pages = page_table.shape
    page = cache_k.shape[2]
    groups = total_pages // n_pages
    ci = lax.broadcasted_iota(jnp.int32, (LANES, LANES), 0)
    co = lax.broadcasted_iota(jnp.int32, (LANES, LANES), 1)
    same_head = (ci % FOX_HEADS) == (co % FOX_HEADS)
    m_within = (same_head & (ci // FOX_HEADS > co // FOX_HEADS)).astype(BF16)
    m_total = same_head.astype(BF16)

    def page_map(nd):
        def make(g):
            return lambda b, j, pt: (layer, pt[b, (groups - 1 - j) * n_pages + g]) + (0,) * nd
        return make

    seq3 = lambda b, j, pt: (b, 0, 0)
    const2 = lambda b, j, pt: (0, 0)
    kv_specs = [pl.BlockSpec((None, None, page, FOX_HEADS, FOX_DH), page_map(3)(g)) for g in range(n_pages)]
    lf_specs = [pl.BlockSpec((None, None, FOX_HEADS, LANES), page_map(2)(g)) for g in range(n_pages)]
    head_spec = pl.BlockSpec((None, FOX_HEADS, FOX_DH), seq3)
    grid_spec = pltpu.PrefetchScalarGridSpec(
        num_scalar_prefetch=1,
        grid=(db, groups),
        in_specs=[head_spec, head_spec, head_spec, pl.BlockSpec((None, 1, LANES), seq3),
                  pl.BlockSpec((LANES, LANES), const2), pl.BlockSpec((LANES, LANES), const2)]
                 + kv_specs + kv_specs + lf_specs,
        out_specs=head_spec,
        scratch_shapes=[pltpu.VMEM((FOX_HEADS, LANES), F32), pltpu.VMEM((FOX_HEADS, LANES), F32),
                        pltpu.VMEM((FOX_HEADS, FOX_DH), F32), pltpu.VMEM((FOX_HEADS, LANES), F32)],
    )
    return pl.pallas_call(
        functools.partial(_decode_body, n_pages=n_pages),
        grid_spec=grid_spec,
        out_shape=jax.ShapeDtypeStruct((db, FOX_HEADS, FOX_DH), F32),
        compiler_params=_cparams(("parallel", "arbitrary")),
        name="fox_decode",
    )(page_table, q, knew, vnew, lfnew, m_within, m_total,
      *([cache_k] * n_pages), *([cache_v] * n_pages), *([cache_lf] * n_pages))


def _merge_body(h_ref, og_ref, gg_ref, of_ref, gn_ref, wo_ref, o_ref):
    og = og_ref[...]
    gg = gg_ref[...]
    gn = gn_ref[...]
    parts = []
    for hd in range(GLA_HEADS):
        oh = og[:, hd * GLA_DV:(hd + 1) * GLA_DV]
        on = oh * lax.rsqrt(jnp.mean(oh * oh, axis=-1, keepdims=True) + RMS_EPS) * gn
        gh = gg[:, hd * GLA_DV:(hd + 1) * GLA_DV]
        parts.append((on * (gh * jax.nn.sigmoid(gh))).astype(BF16))
    gla_part = jnp.concatenate(parts, axis=-1)
    mixed = (jnp.dot(gla_part, wo_ref[:D_GLA, :], preferred_element_type=F32)
             + jnp.dot(of_ref[...], wo_ref[D_GLA:, :], preferred_element_type=F32))
    o_ref[...] = h_ref[...] + mixed


def _merge(h, o_gla, gg, o_fox, gn, wo, tm):
    rows, d = h.shape
    row_spec = lambda w: pl.BlockSpec((tm, w), lambda i: (i, 0))
    return pl.pallas_call(
        _merge_body,
        grid=(rows // tm,),
        in_specs=[row_spec(d), row_spec(D_GLA), row_spec(D_GLA), row_spec(D_FOX),
                  _resident(gn.shape), _resident(wo.shape)],
        out_specs=row_spec(d),
        out_shape=jax.ShapeDtypeStruct((rows, d), F32),
        compiler_params=_cparams(("parallel",)),
        name="merge",
    )(h, o_gla, gg, o_fox, gn, wo)


def _norm_body(h_ref, nw_ref, o_ref):
    o_ref[...] = _rms(h_ref[...], nw_ref[...])


def _norm(h, nw, tm):
    rows, d = h.shape
    return pl.pallas_call(
        _norm_body,
        grid=(rows // tm,),
        in_specs=[pl.BlockSpec((tm, d), lambda i: (i, 0)), _resident((1, d))],
        out_specs=pl.BlockSpec((tm, d), lambda i: (i, 0)),
        out_shape=jax.ShapeDtypeStruct((rows, d), F32),
        compiler_params=_cparams(("parallel",)),
        name="final_norm",
    )(h, nw)


def _pick(n, prefs):
    for t in prefs:
        if n % t == 0:
            return t
    raise ValueError(f"no tile in {prefs} divides {n}")


def _pack_w_in(w_in_l, b_f_l, w_a2_l, b_a2_l):
    sizes = (GLA_QK, GLA_QK, D_GLA, D_GLA, GLA_RANK, D_FOX, D_FOX, D_FOX, FOX_HEADS)
    offs = [0]
    for s in sizes:
        offs.append(offs[-1] + s)
    col = lambda i: w_in_l[:, offs[i]:offs[i + 1]]
    wmain = jnp.concatenate([col(0), col(1), col(2), col(3), col(5), col(6), col(7)], axis=1).astype(BF16)
    d = w_in_l.shape[0]
    pad = LANES - FOX_HEADS - GLA_RANK
    wsmall = jnp.concatenate([col(8), col(4), jnp.zeros((d, pad), F32)], axis=1).astype(BF16)
    bsm = jnp.concatenate([b_f_l, jnp.zeros((LANES - FOX_HEADS,), F32)])[None, :]
    wa2 = jnp.concatenate([jnp.zeros((FOX_HEADS, GLA_QK), F32), w_a2_l, jnp.zeros((pad, GLA_QK), F32)],
                          axis=0).astype(BF16)
    return wmain, wsmall, wa2, b_a2_l[None, :], bsm


def kernel(x_prompt, x_sample, cache_fox_k, cache_fox_v, cache_fox_logf, state_gla, page_table, meta_tokens,
           ffn1_norm, ffn1_w_up, ffn1_w_down, mix_norm, w_in, fox_b_f, gla_w_a2, gla_b_a2, gla_norm, w_out,
           ffn2_norm, ffn2_w_up, ffn2_w_down, final_norm):
    b, seq, d = x_prompt.shape
    db, dec_seq, _ = x_sample.shape
    n_meta = meta_tokens.shape[0]
    depth = w_in.shape[0]
    n_pool, page = cache_fox_k.shape[1], cache_fox_k.shape[2]
    assert dec_seq == 1 and n_meta + db <= SIDE_ROWS and page == LANES
    s0, s1 = n_meta, n_meta + db

    tm = _pick(seq, (512, 256, 128))
    tq = _pick(seq, (1024, 512, 256, 128))
    cs = _pick(seq, (64,))
    pages_per_step = _pick(page_table.shape[1], (8, 4, 2, 1))

    hm = x_prompt.reshape(b * seq, d)
    side = jnp.concatenate([meta_tokens, x_sample[:, 0, :], jnp.zeros((SIDE_ROWS - s1, d), F32)], axis=0)

    cache_lf = cache_fox_logf.reshape(depth, n_pool, FOX_HEADS, LANES)

    outs = {n: [] for n in ("pk", "pv", "plf", "ps", "sk", "sv", "slf", "ss")}
    for l in range(depth):
        up1, dn1 = ffn1_w_up[l].astype(BF16), ffn1_w_down[l].astype(BF16)
        up2, dn2 = ffn2_w_up[l].astype(BF16), ffn2_w_down[l].astype(BF16)
        wmain, wsmall, wa2, ba2, bsm = _pack_w_in(w_in[l], fox_b_f[l], gla_w_a2[l], gla_b_a2[l])
        wo = w_out[l].astype(BF16)

        hm = _ffn(hm, ffn1_norm[l][None, :], up1, dn1, tm)
        side = _ffn(side, ffn1_norm[l][None, :], up1, dn1, SIDE_ROWS)
        pm = _proj(hm, mix_norm[l][None, :], wmain, wsmall, wa2, ba2, bsm, tm, seq)
        sd = _proj(side, mix_norm[l][None, :], wmain, wsmall, wa2, ba2, bsm, SIDE_ROWS, SIDE_ROWS,
                   prefix_rows=n_meta)

        zero_state = jnp.zeros((1, 2, 2 * GLA_DV, LANES), F32)
        o_gla_meta, st_meta = _gla(sd["gq"][None], sd["gk"][None], sd["gv"][None], sd["glog"][None],
                                   zero_state, SIDE_ROWS, n_valid=n_meta)
        r3 = lambda a: a.reshape(b, seq, a.shape[-1])
        o_gla, st_prompt = _gla(r3(pm["gq"]), r3(pm["gk"]), r3(pm["gv"]), r3(pm["glog"]), st_meta, cs)
        pad_tok = lambda a: jnp.pad(a[s0:s1][:, None, :], ((0, 0), (0, SIDE_ROWS - 1), (0, 0)))
        o_gla_s, st_sample = _gla(pad_tok(sd["gq"]), pad_tok(sd["gk"]), pad_tok(sd["gv"]), pad_tok(sd["glog"]),
                                  _state_to_pairs(state_gla[l]), SIDE_ROWS, n_valid=1)

        o_fox_meta = _fox(sd["qa"][None], sd["ka"][None], sd["va"][None], SIDE_ROWS)
        o_fox = _fox(r3(pm["qa"]), r3(pm["ka"]), r3(pm["va"]), tq, prefix=(sd["ka"], sd["va"]))

        hd3 = lambda a: a[s0:s1].reshape(db, FOX_HEADS, FOX_DH)
        q_s = sd["qa"][s0:s1].reshape(db, FOX_HEADS, LANES)[:, :, :FOX_DH].astype(F32)
        lf_new = jnp.tile(sd["lf"][s0:s1], (1, LANES // FOX_HEADS))[:, None, :]
        o_fox_s = _decode(l, page_table, q_s, hd3(sd["fk"]), hd3(sd["fv"]), lf_new,
                          cache_fox_k, cache_fox_v, cache_lf, pages_per_step).reshape(db, 1, D_FOX)

        tail = jnp.zeros((SIDE_ROWS - s1, D_GLA), F32)
        og_side = jnp.concatenate([o_gla_meta[0, :n_meta], o_gla_s[:, 0, :], tail], axis=0)
        of_side = jnp.concatenate([o_fox_meta[0, :n_meta], o_fox_s[:, 0, :].astype(BF16), tail.astype(BF16)], axis=0)
        gn = gla_norm[l][None, :]
        hm = _merge(hm, o_gla.reshape(b * seq, D_GLA), pm["gg"], o_fox.reshape(b * seq, D_FOX), gn, wo, tm)
        side = _merge(side, og_side, sd["gg"], of_side, gn, wo, SIDE_ROWS)
        hm = _ffn(hm, ffn2_norm[l][None, :], up2, dn2, tm)
        side = _ffn(side, ffn2_norm[l][None, :], up2, dn2, SIDE_ROWS)

        heads = lambda a: a.reshape(a.shape[:-1] + (FOX_HEADS, FOX_DH))
        with_meta = lambda m, r: jnp.concatenate(
            [jnp.broadcast_to(m[None, :n_meta], (b, n_meta) + m.shape[1:]), r.reshape((b, seq) + r.shape[1:])], axis=1)
        outs["pk"].append(heads(with_meta(sd["fk"], pm["fk"])))
        outs["pv"].append(heads(with_meta(sd["fv"], pm["fv"])))
        outs["plf"].append(with_meta(sd["lf"], pm["lf"]))
        outs["ps"].append(_pairs_to_state(st_prompt))
        outs["sk"].append(heads(sd["fk"][s0:s1][:, None, :]))
        outs["sv"].append(heads(sd["fv"][s0:s1][:, None, :]))
        outs["slf"].append(sd["lf"][s0:s1][:, None, :])
        outs["ss"].append(_pairs_to_state(st_sample))

    y_prompt = _norm(hm, final_norm[None, :], tm).reshape(b, seq, d)
    y_sample = _norm(side, final_norm[None, :], SIDE_ROWS)[s0:s1][:, None, :]
    st = lambda n: jnp.stack(outs[n])
    return (y_prompt, y_sample, st("pk"), st("pv"), st("plf"), st("ps"), st("sk"), st("sv"), st("slf"), st("ss"))
```

```python
import functools

import jax
import jax.numpy as jnp
from jax import lax
from jax.experimental import pallas as pl
from jax.experimental.pallas import tpu as pltpu

F32 = jnp.float32
BF16 = jnp.bfloat16

LANES = 128
SIDE_ROWS = 128
RMS_EPS = 1e-6
NEG = -1e30
GLA_HEADS, GLA_DK, GLA_DV, GLA_RANK = 4, 64, 128, 16
GLA_GATE_NORM = 16.0
FOX_HEADS, FOX_DH = 8, 64
D_GLA = GLA_HEADS * GLA_DV
D_FOX = FOX_HEADS * FOX_DH
GLA_QK = GLA_HEADS * GLA_DK
VMEM_LIMIT = 56 * 1024 * 1024

NT_DIMS = (((1,), (1,)), ((), ()))
TN_DIMS = (((0,), (0,)), ((), ()))


def _cparams(sem):
    return pltpu.CompilerParams(dimension_semantics=sem, vmem_limit_bytes=VMEM_LIMIT)


def _resident(shape):
    nd = len(shape)
    return pl.BlockSpec(shape, lambda *_: (0,) * nd, pipeline_mode=pl.Buffered(1))


def _rms(x, w):
    return x * lax.rsqrt(jnp.mean(x * x, axis=-1, keepdims=True) + RMS_EPS) * w


def _logsig(x):
    return jnp.minimum(x, 0.0) - jnp.log1p(jnp.exp(-jnp.abs(x)))


def _split3(x):
    hi = x.astype(BF16)
    r1 = x - hi.astype(F32)
    mid = r1.astype(BF16)
    lo = (r1 - mid.astype(F32)).astype(BF16)
    return hi, mid, lo


def _dot01_right(x, m01):
    hi, mid, lo = _split3(x)
    d = lambda a: jnp.dot(a, m01, preferred_element_type=F32)
    return (d(lo) + d(mid)) + d(hi)


def _dot01_left(m01, x):
    hi, mid, lo = _split3(x)
    d = lambda a: jnp.dot(m01, a, preferred_element_type=F32)
    return (d(lo) + d(mid)) + d(hi)


def _ffn_body(h_ref, nw_ref, wup_ref, wdn_ref, o_ref, *, ff, ck):
    x = h_ref[...]
    xn = _rms(x, nw_ref[...]).astype(BF16)
    acc = jnp.zeros(x.shape, F32)
    for c in range(ff // ck):
        g = jnp.dot(xn, wup_ref[:, c * ck:(c + 1) * ck], preferred_element_type=F32)
        u = jnp.dot(xn, wup_ref[:, ff + c * ck:ff + (c + 1) * ck], preferred_element_type=F32)
        hm = (g * jax.nn.sigmoid(g) * u).astype(BF16)
        acc = acc + jnp.dot(hm, wdn_ref[c * ck:(c + 1) * ck, :], preferred_element_type=F32)
    o_ref[...] = x + 0.5 * acc


def _ffn(h, nw, wup, wdn, tm):
    rows, d = h.shape
    ff = wdn.shape[0]
    ck = 256 if ff % 256 == 0 else LANES
    return pl.pallas_call(
        functools.partial(_ffn_body, ff=ff, ck=ck),
        grid=(rows // tm,),
        in_specs=[pl.BlockSpec((tm, d), lambda i: (i, 0)),
                  _resident((1, d)), _resident(wup.shape), _resident(wdn.shape)],
        out_specs=pl.BlockSpec((tm, d), lambda i: (i, 0)),
        out_shape=jax.ShapeDtypeStruct((rows, d), F32),
        compiler_params=_cparams(("parallel",)),
        name="ffn",
    )(h, nw, wup, wdn)


def _proj_body(h_ref, nw_ref, wmain_ref, wsmall_ref, wa2_ref, ba2_ref, bsm_ref, tri_ref, place_ref,
               gq_ref, gk_ref, gv_ref, gg_ref, glog_ref, qa_ref, ka_ref, va_ref,
               fk_ref, fv_ref, lf_ref, carry_ref, *, tiles_per_seq, prefix_rows):
    @pl.when(pl.program_id(0) % tiles_per_seq == 0)
    def _():
        carry_ref[...] = jnp.zeros_like(carry_ref)

    x = h_ref[...]
    tm = x.shape[0]
    xn = _rms(x, nw_ref[...]).astype(BF16)

    def seg(i):
        return jnp.dot(xn, wmain_ref[:, i * 512:(i + 1) * 512], preferred_element_type=F32)

    z = seg(0)
    gq_ref[...] = z[:, :GLA_QK] * (GLA_DK ** -0.5)
    gk_ref[...] = z[:, GLA_QK:]
    gv_ref[...] = seg(1).astype(BF16)
    gg_ref[...] = seg(2)

    zs = jnp.dot(xn, wsmall_ref[...], preferred_element_type=F32)
    lfull = _logsig(zs + bsm_ref[...])
    lf_ref[...] = lfull[:, :FOX_HEADS]
    x2 = jnp.dot(zs.astype(BF16), wa2_ref[...], preferred_element_type=F32) + ba2_ref[...]
    glog_ref[...] = _logsig(x2) * (1.0 / GLA_GATE_NORM)

    c = _dot01_left(tri_ref[...], lfull) + carry_ref[0:1, :]
    carry_ref[...] = jnp.broadcast_to(c[tm - 1:tm, :], carry_ref.shape)
    if prefix_rows is None:
        bias = -c
    else:
        rows = lax.broadcasted_iota(jnp.int32, (tm, 1), 0)
        bias = jnp.where(rows < prefix_rows, c[prefix_rows - 1:prefix_rows, :] - c, NEG)
    b_hi, b_mid, b_lo = _split3(bias)
    place = lambda part, i: jnp.dot(part, place_ref[i], preferred_element_type=F32)
    baug = place(b_hi, 0) + place(b_mid, 1) + place(b_lo, 2)

    lane = lax.broadcasted_iota(jnp.int32, (1, LANES), 1)
    data = lane < FOX_DH
    q_aug = jnp.where(lane < FOX_DH + 3, 1.0, 0.0)
    zq = seg(3) * (FOX_DH ** -0.5)
    zk = seg(4)
    zv = seg(5)
    fk_ref[...] = zk
    fv_ref[...] = zv
    for p in range(FOX_HEADS // 2):
        sl = slice(p * LANES, (p + 1) * LANES)
        for hh in range(2):
            h = 2 * p + hh
            hs = slice(h * LANES, (h + 1) * LANES)
            pick = (lambda t: t) if hh == 0 else (lambda t: pltpu.roll(t, FOX_DH, 1))
            qa_ref[:, hs] = jnp.where(data, pick(zq[:, sl]), q_aug).astype(BF16)
            ka_ref[:, hs] = jnp.where(data, pick(zk[:, sl]), baug[:, hs]).astype(BF16)
            va_ref[:, hs] = jnp.where(data, pick(zv[:, sl]), 1.0).astype(BF16)


def _proj(h, nw, wmain, wsmall, wa2, ba2, bsm, tm, rows_per_seq, prefix_rows=None):
    rows, d = h.shape
    tri = (lax.broadcasted_iota(jnp.int32, (tm, tm), 0)
           >= lax.broadcasted_iota(jnp.int32, (tm, tm), 1)).astype(BF16)
    src_row = lax.broadcasted_iota(jnp.int32, (3, LANES, FOX_HEADS * LANES), 1)
    dst = lax.broadcasted_iota(jnp.int32, (3, LANES, FOX_HEADS * LANES), 2)
    part = lax.broadcasted_iota(jnp.int32, (3, LANES, FOX_HEADS * LANES), 0)
    place = ((dst == src_row * LANES + FOX_DH + part) & (src_row < FOX_HEADS)).astype(BF16)
    row_spec = lambda w: pl.BlockSpec((tm, w), lambda i: (i, 0))
    sds = lambda w, dt: jax.ShapeDtypeStruct((rows, w), dt)
    aug = FOX_HEADS * LANES
    outs = pl.pallas_call(
        functools.partial(_proj_body, tiles_per_seq=rows_per_seq // tm, prefix_rows=prefix_rows),
        grid=(rows // tm,),
        in_specs=[row_spec(d), _resident((1, d)), _resident(wmain.shape), _resident(wsmall.shape),
                  _resident(wa2.shape), _resident(ba2.shape), _resident(bsm.shape), _resident(tri.shape),
                  _resident(place.shape)],
        out_specs=[row_spec(GLA_QK), row_spec(GLA_QK), row_spec(D_GLA), row_spec(D_GLA), row_spec(GLA_QK),
                   row_spec(aug), row_spec(aug), row_spec(aug), row_spec(D_FOX), row_spec(D_FOX),
                   row_spec(FOX_HEADS)],
        out_shape=[sds(GLA_QK, F32), sds(GLA_QK, F32), sds(D_GLA, BF16), sds(D_GLA, F32), sds(GLA_QK, F32),
                   sds(aug, BF16), sds(aug, BF16), sds(aug, BF16), sds(D_FOX, F32), sds(D_FOX, F32),
                   sds(FOX_HEADS, F32)],
        scratch_shapes=[pltpu.VMEM((8, LANES), F32)],
        compiler_params=_cparams(("arbitrary",)),
        name="proj",
    )(h, nw, wmain, wsmall, wa2, ba2, bsm, tri, place)
    keys = ("gq", "gk", "gv", "gg", "glog", "qa", "ka", "va", "fk", "fv", "lf")
    return dict(zip(keys, outs))


def _gla_body(q_ref, k_ref, v_ref, lg_ref, s0_ref, o_ref, st_ref, st_scr, *, n_valid):
    c = pl.program_id(1)

    @pl.when(c == 0)
    def _():
        st_scr[...] = jnp.broadcast_to(s0_ref[...], st_scr.shape)

    bb, cs = q_ref.shape[0], q_ref.shape[1]
    row = lax.broadcasted_iota(jnp.int32, (cs, cs), 0)
    col = lax.broadcasted_iota(jnp.int32, (cs, cs), 1)
    tri = row >= col
    tri_b = tri.astype(BF16)
    lane = lax.broadcasted_iota(jnp.int32, (1, LANES), 1)
    bd = ((lax.broadcasted_iota(jnp.int32, (2 * GLA_DV, LANES), 0) < GLA_DV)
          == (lax.broadcasted_iota(jnp.int32, (2 * GLA_DV, LANES), 1) < GLA_DK))
    if n_valid is not None:
        valid = lax.broadcasted_iota(jnp.int32, (cs, 1), 0) < n_valid

    for b in range(bb):
        for p in range(2):
            ks = slice(p * LANES, (p + 1) * LANES)
            vs = slice(p * 2 * GLA_DV, (p + 1) * 2 * GLA_DV)
            q = q_ref[b, :, ks]
            k = k_ref[b, :, ks]
            lg = lg_ref[b, :, ks]
            v = v_ref[b, :, vs]
            if n_valid is not None:
                k = jnp.where(valid, k, 0.0)
                lg = jnp.where(valid, lg, 0.0)

            g = _dot01_left(tri_b, lg)
            g_last = g[cs - 1:cs, :]
            g_mid = g[cs // 2 - 1:cs // 2, :]
            qg = (q * jnp.exp(g - g_mid)).astype(BF16)
            kg = (k * jnp.exp(g_mid - g)).astype(BF16)
            kd = (k * jnp.exp(g_last - g)).astype(BF16)
            qi = (q * jnp.exp(g)).astype(BF16)

            st = st_scr[b, p]
            o_inter = lax.dot_general(qi, st.astype(BF16), NT_DIMS, preferred_element_type=F32)
            for hh in range(2):
                head = (lane < GLA_DK) if hh == 0 else (lane >= GLA_DK)
                a = lax.dot_general(jnp.where(head, qg, jnp.zeros_like(qg)), kg, NT_DIMS,
                                    preferred_element_type=F32)
                a = jnp.where(tri, a, 0.0).astype(BF16)
                hv = slice(hh * GLA_DV, (hh + 1) * GLA_DV)
                o_intra = jnp.dot(a, v[:, hv], preferred_element_type=F32)
                o_ref[b, :, p * 2 * GLA_DV + hh * GLA_DV:p * 2 * GLA_DV + (hh + 1) * GLA_DV] = o_inter[:, hv] + o_intra

            upd = lax.dot_general(v, kd, TN_DIMS, preferred_element_type=F32)
            st_new = st * jnp.exp(g_last) + jnp.where(bd, upd, 0.0)
            st_scr[b, p] = st_new

    @pl.when(c == pl.num_programs(1) - 1)
    def _():
        st_ref[...] = st_scr[...]


def _gla(q, k, v, lg, s0t, cs, bb, n_valid=None):
    bg, t, _ = q.shape
    qk_spec = pl.BlockSpec((bb, cs, GLA_QK), lambda i, c: (i, c, 0))
    v_spec = pl.BlockSpec((bb, cs, D_GLA), lambda i, c: (i, c, 0))
    st_shape = (bb, 2, 2 * GLA_DV, LANES)
    if s0t.shape[0] == 1:
        s0_spec = pl.BlockSpec((1,) + st_shape[1:], lambda i, c: (0, 0, 0, 0))
    else:
        s0_spec = pl.BlockSpec(st_shape, lambda i, c: (i, 0, 0, 0))
    return pl.pallas_call(
        functools.partial(_gla_body, n_valid=n_valid),
        grid=(bg // bb, t // cs),
        in_specs=[qk_spec, qk_spec, v_spec, qk_spec, s0_spec],
        out_specs=[v_spec, pl.BlockSpec(st_shape, lambda i, c: (i, 0, 0, 0))],
        out_shape=[jax.ShapeDtypeStruct((bg, t, D_GLA), F32),
                   jax.ShapeDtypeStruct((bg, 2, 2 * GLA_DV, LANES), F32)],
        scratch_shapes=[pltpu.VMEM(st_shape, F32)],
        compiler_params=_cparams(("parallel", "arbitrary")),
        name="gla",
    )(q, k, v, lg, s0t)


def _state_to_pairs(s):
    bs = s.shape[0]
    st = jnp.swapaxes(s, -1, -2).reshape(bs, 2, 2, GLA_DV, GLA_DK)
    z = jnp.zeros_like(st[:, :, 0])
    top = jnp.concatenate([st[:, :, 0], z], axis=-1)
    bot = jnp.concatenate([z, st[:, :, 1]], axis=-1)
    return jnp.concatenate([top, bot], axis=-2)


def _pairs_to_state(st):
    a = st[:, :, :GLA_DV, :GLA_DK]
    b = st[:, :, GLA_DV:, GLA_DK:]
    s = jnp.stack([a, b], axis=2)
    return jnp.swapaxes(s, -1, -2).reshape(st.shape[0], GLA_HEADS, GLA_DK, GLA_DV)


def _fox_body(*refs, has_prefix):
    if has_prefix:
        q_ref, k_ref, v_ref, km_ref, vm_ref, o_ref, m_scr, acc_scr = refs
    else:
        q_ref, k_ref, v_ref, o_ref, m_scr, acc_scr = refs
    i = pl.program_id(2)
    tq = q_ref.shape[0]
    m_scr[...] = jnp.full(m_scr.shape, NEG, F32)
    acc_scr[...] = jnp.zeros(acc_scr.shape, F32)

    def tile(kfn, vfn, mask):
        for hh in range(2):
            hs = slice(hh * LANES, (hh + 1) * LANES)
            s = lax.dot_general(q_ref[:, hs], kfn(hs), NT_DIMS, preferred_element_type=F32)
            if mask is not None:
                s = jnp.where(mask, s, NEG)
            nk = s.shape[1]
            mx = s[:, :LANES]
            for t in range(1, nk // LANES):
                mx = jnp.maximum(mx, s[:, t * LANES:(t + 1) * LANES])
            m_old = m_scr[hh]
            m_new = jnp.maximum(m_old, jnp.max(mx, axis=-1, keepdims=True))
            p = jnp.exp(s - jnp.tile(m_new, (1, nk // LANES)))
            alpha = jnp.exp(m_old - m_new)
            acc_scr[hh] = alpha * acc_scr[hh] + jnp.dot(p.astype(BF16), vfn(hs), preferred_element_type=F32)
            m_scr[hh] = m_new

    if has_prefix:
        tile(lambda hs: km_ref[:, hs], lambda hs: vm_ref[:, hs], None)

    def body(j, carry):
        off = pl.multiple_of(j * tq, tq)
        tile(lambda hs: k_ref[pl.ds(off, tq), hs], lambda hs: v_ref[pl.ds(off, tq), hs], None)
        return carry

    lax.fori_loop(0, i, body, 0)
    off = pl.multiple_of(i * tq, tq)
    causal = (lax.broadcasted_iota(jnp.int32, (tq, tq), 0) >= lax.broadcasted_iota(jnp.int32, (tq, tq), 1))
    tile(lambda hs: k_ref[pl.ds(off, tq), hs], lambda hs: v_ref[pl.ds(off, tq), hs], causal)

    outs = []
    for hh in range(2):
        acc = acc_scr[hh]
        outs.append(acc / pltpu.roll(acc, FOX_DH, 1))
    first = lax.broadcasted_iota(jnp.int32, (1, LANES), 1) < FOX_DH
    o_ref[...] = jnp.where(first, outs[0], pltpu.roll(outs[1], FOX_DH, 1)).astype(BF16)


def _fox(qa, ka, va, tq, prefix=None):
    b, l, _ = qa.shape
    q_spec = pl.BlockSpec((None, tq, 2 * LANES), lambda bb, p, i: (bb, i, p))
    kv_spec = pl.BlockSpec((None, l, 2 * LANES), lambda bb, p, i: (bb, 0, p))
    in_specs = [q_spec, kv_spec, kv_spec]
    args = [qa, ka, va]
    if prefix is not None:
        pk_spec = pl.BlockSpec((SIDE_ROWS, 2 * LANES), lambda bb, p, i: (0, p))
        in_specs += [pk_spec, pk_spec]
        args += list(prefix)
    return pl.pallas_call(
        functools.partial(_fox_body, has_prefix=prefix is not None),
        grid=(b, FOX_HEADS // 2, l // tq),
        in_specs=in_specs,
        out_specs=pl.BlockSpec((None, tq, LANES), lambda bb, p, i: (bb, i, p)),
        out_shape=jax.ShapeDtypeStruct((b, l, D_FOX), BF16),
        scratch_shapes=[pltpu.VMEM((2, tq, LANES), F32), pltpu.VMEM((2, tq, LANES), F32)],
        compiler_params=_cparams(("parallel", "parallel", "arbitrary")),
        name="fox",
    )(*args)


def _decode_body(pt_ref, qbd_ref, q_ref, knew_ref, vnew_ref, lfnew_ref, u_ref, *rest, n_pages):
    del pt_ref
    k_refs = rest[:n_pages]
    v_refs = rest[n_pages:2 * n_pages]
    lf_refs = rest[2 * n_pages:3 * n_pages]
    o_ref, m_scr, l_scr, acc_scr, carry_scr = rest[3 * n_pages:]
    j = pl.program_id(1)
    qbd = qbd_ref[...]
    width = qbd.shape[1]
    page = u_ref.shape[0]
    head_lane = (lax.broadcasted_iota(jnp.int32, (FOX_HEADS, width), 1) // FOX_DH
                 == lax.broadcasted_iota(jnp.int32, (FOX_HEADS, width), 0))

    @pl.when(j == 0)
    def _():
        s_new = jnp.sum(q_ref[...] * knew_ref[...], axis=-1, keepdims=True)
        m_scr[...] = jnp.broadcast_to(s_new, m_scr.shape)
        l_scr[...] = jnp.ones(l_scr.shape, F32)
        acc_scr[...] = jnp.where(head_lane, jnp.broadcast_to(vnew_ref[...], acc_scr.shape), 0.0)
        carry_scr[...] = jnp.zeros(carry_scr.shape, F32)

    lf_new = lfnew_ref[...]
    carry = carry_scr[...] + lf_new
    scores = []
    for g in reversed(range(n_pages)):
        lft = lf_refs[g][...]
        suffix = _dot01_right(lft, u_ref[...])
        kt = k_refs[g][...].reshape(width, page).astype(BF16)
        scores.append(jnp.dot(qbd, kt, preferred_element_type=F32) + (suffix + carry))
        carry = carry + jnp.sum(lft, axis=-1, keepdims=True)
    carry_scr[...] = carry - lf_new

    mx = scores[0]
    for s in scores[1:]:
        mx = jnp.maximum(mx, s)
    m_old = m_scr[...]
    m_new = jnp.maximum(m_old, jnp.max(mx, axis=-1, keepdims=True))
    alpha = jnp.exp(m_old - m_new)
    psum = jnp.zeros(m_old.shape, F32)
    pv = jnp.zeros(acc_scr.shape, F32)
    for s, g in zip(scores, reversed(range(n_pages))):
        p = jnp.exp(s - m_new)
        psum = psum + p
        vt = v_refs[g][...].reshape(width, page).astype(BF16)
        pv = pv + lax.dot_general(p.astype(BF16), vt, NT_DIMS, preferred_element_type=F32)
    l_scr[...] = alpha * l_scr[...] + jnp.sum(psum, axis=-1, keepdims=True)
    acc_scr[...] = alpha[:, 0:1] * acc_scr[...] + pv
    m_scr[...] = m_new

    @pl.when(j == pl.num_programs(1) - 1)
    def _():
        o = jnp.where(head_lane, acc_scr[...] / l_scr[:, 0:1], 0.0)
        o_ref[...] = jnp.sum(o, axis=0, keepdims=True)


def _decode(layer, page_table, qbd, q, knew, vnew, lfnew, cache_kt, cache_vt, cache_lft, n_pages):
    db, total_pages = page_table.shape
    page = cache_kt.shape[-1]
    groups = total_pages // n_pages
    strict_upper = (lax.broadcasted_iota(jnp.int32, (page, page), 0)
                    > lax.broadcasted_iota(jnp.int32, (page, page), 1)).astype(BF16)

    def page_map(nd):
        def make(g):
            return lambda b, j, pt: (layer, pt[b, (groups - 1 - j) * n_pages + g]) + (0,) * nd
        return make

    seq3 = lambda b, j, pt: (b, 0, 0)
    kv_specs = [pl.BlockSpec((None, None, FOX_HEADS, FOX_DH, page), page_map(3)(g)) for g in range(n_pages)]
    lf_specs = [pl.BlockSpec((None, None, FOX_HEADS, page), page_map(2)(g)) for g in range(n_pages)]
    head_spec = pl.BlockSpec((None, FOX_HEADS, FOX_DH), seq3)
    flat_spec = pl.BlockSpec((None, 1, D_FOX), seq3)
    grid_spec = pltpu.PrefetchScalarGridSpec(
        num_scalar_prefetch=1,
        grid=(db, groups),
        in_specs=[pl.BlockSpec((None, FOX_HEADS, D_FOX), seq3), head_spec, head_spec, flat_spec,
                  pl.BlockSpec((None, FOX_HEADS, 1), seq3), pl.BlockSpec((page, page), lambda b, j, pt: (0, 0))]
                 + kv_specs + kv_specs + lf_specs,
        out_specs=flat_spec,
        scratch_shapes=[pltpu.VMEM((FOX_HEADS, LANES), F32), pltpu.VMEM((FOX_HEADS, LANES), F32),
                        pltpu.VMEM((FOX_HEADS, D_FOX), F32), pltpu.VMEM((FOX_HEADS, LANES), F32)],
    )
    return pl.pallas_call(
        functools.partial(_decode_body, n_pages=n_pages),
        grid_spec=grid_spec,
        out_shape=jax.ShapeDtypeStruct((db, 1, D_FOX), F32),
        compiler_params=_cparams(("parallel", "arbitrary")),
        name="fox_decode",
    )(page_table, qbd, q, knew, vnew, lfnew, strict_upper,
      *([cache_kt] * n_pages), *([cache_vt] * n_pages), *([cache_lft] * n_pages))


def _merge_body(h_ref, og_ref, gg_ref, of_ref, gn_ref, wo_ref, o_ref):
    og = og_ref[...]
    gg = gg_ref[...]
    gn = gn_ref[...]
    parts = []
    for hd in range(GLA_HEADS):
        oh = og[:, hd * GLA_DV:(hd + 1) * GLA_DV]
        on = oh * lax.rsqrt(jnp.mean(oh * oh, axis=-1, keepdims=True) + RMS_EPS) * gn
        gh = gg[:, hd * GLA_DV:(hd + 1) * GLA_DV]
        parts.append((on * (gh * jax.nn.sigmoid(gh))).astype(BF16))
    gla_part = jnp.concatenate(parts, axis=-1)
    mixed = (jnp.dot(gla_part, wo_ref[:D_GLA, :], preferred_element_type=F32)
             + jnp.dot(of_ref[...], wo_ref[D_GLA:, :], preferred_element_type=F32))
    o_ref[...] = h_ref[...] + mixed


def _merge(h, o_gla, gg, o_fox, gn, wo, tm):
    rows, d = h.shape
    row_spec = lambda w: pl.BlockSpec((tm, w), lambda i: (i, 0))
    return pl.pallas_call(
        _merge_body,
        grid=(rows // tm,),
        in_specs=[row_spec(d), row_spec(D_GLA), row_spec(D_GLA), row_spec(D_FOX),
                  _resident(gn.shape), _resident(wo.shape)],
        out_specs=row_spec(d),
        out_shape=jax.ShapeDtypeStruct((rows, d), F32),
        compiler_params=_cparams(("parallel",)),
        name="merge",
    )(h, o_gla, gg, o_fox, gn, wo)


def _norm_body(h_ref, nw_ref, o_ref):
    o_ref[...] = _rms(h_ref[...], nw_ref[...])


def _norm(h, nw, tm):
    rows, d = h.shape
    return pl.pallas_call(
        _norm_body,
        grid=(rows // tm,),
        in_specs=[pl.BlockSpec((tm, d), lambda i: (i, 0)), _resident((1, d))],
        out_specs=pl.BlockSpec((tm, d), lambda i: (i, 0)),
        out_shape=jax.ShapeDtypeStruct((rows, d), F32),
        compiler_params=_cparams(("parallel",)),
        name="final_norm",
    )(h, nw)


def _pick(n, prefs):
    for t in prefs:
        if n % t == 0:
            return t
    raise ValueError(f"no tile in {prefs} divides {n}")


def _pack_w_in(w_in_l, b_f_l, w_a2_l, b_a2_l):
    sizes = (GLA_QK, GLA_QK, D_GLA, D_GLA, GLA_RANK, D_FOX, D_FOX, D_FOX, FOX_HEADS)
    offs = [0]
    for s in sizes:
        offs.append(offs[-1] + s)
    col = lambda i: w_in_l[:, offs[i]:offs[i + 1]]
    wmain = jnp.concatenate([col(0), col(1), col(2), col(3), col(5), col(6), col(7)], axis=1).astype(BF16)
    d = w_in_l.shape[0]
    pad = LANES - FOX_HEADS - GLA_RANK
    wsmall = jnp.concatenate([col(8), col(4), jnp.zeros((d, pad), F32)], axis=1).astype(BF16)
    bsm = jnp.concatenate([b_f_l, jnp.zeros((LANES - FOX_HEADS,), F32)])[None, :]
    wa2 = jnp.concatenate([jnp.zeros((FOX_HEADS, GLA_QK), F32), w_a2_l, jnp.zeros((pad, GLA_QK), F32)],
                          axis=0).astype(BF16)
    return wmain, wsmall, wa2, b_a2_l[None, :], bsm


def kernel(x_prompt, x_sample, cache_fox_k, cache_fox_v, cache_fox_logf, state_gla, page_table, meta_tokens,
           ffn1_norm, ffn1_w_up, ffn1_w_down, mix_norm, w_in, fox_b_f, gla_w_a2, gla_b_a2, gla_norm, w_out,
           ffn2_norm, ffn2_w_up, ffn2_w_down, final_norm):
    b, seq, d = x_prompt.shape
    db, dec_seq, _ = x_sample.shape
    n_meta = meta_tokens.shape[0]
    depth = w_in.shape[0]
    n_pool, page = cache_fox_k.shape[1], cache_fox_k.shape[2]
    assert dec_seq == 1 and n_meta + db <= SIDE_ROWS and page == LANES
    s0, s1 = n_meta, n_meta + db

    tm = _pick(seq, (512, 256, 128))
    tq = _pick(seq, (1024, 512, 256, 128))
    cs = _pick(seq, (64,))
    pages_per_step = _pick(page_table.shape[1], (8, 4, 2, 1))

    hm = x_prompt.reshape(b * seq, d)
    side = jnp.concatenate([meta_tokens, x_sample[:, 0, :], jnp.zeros((SIDE_ROWS - s1, d), F32)], axis=0)

    cache_kt = jnp.transpose(cache_fox_k, (0, 1, 3, 4, 2))
    cache_vt = jnp.transpose(cache_fox_v, (0, 1, 3, 4, 2))
    cache_lft = jnp.swapaxes(cache_fox_logf, -1, -2)

    outs = {n: [] for n in ("pk", "pv", "plf", "ps", "sk", "sv", "slf", "ss")}
    for l in range(depth):
        up1, dn1 = ffn1_w_up[l].astype(BF16), ffn1_w_down[l].astype(BF16)
        up2, dn2 = ffn2_w_up[l].astype(BF16), ffn2_w_down[l].astype(BF16)
        wmain, wsmall, wa2, ba2, bsm = _pack_w_in(w_in[l], fox_b_f[l], gla_w_a2[l], gla_b_a2[l])
        wo = w_out[l].astype(BF16)

        hm = _ffn(hm, ffn1_norm[l][None, :], up1, dn1, tm)
        side = _ffn(side, ffn1_norm[l][None, :], up1, dn1, SIDE_ROWS)
        pm = _proj(hm, mix_norm[l][None, :], wmain, wsmall, wa2, ba2, bsm, tm, seq)
        sd = _proj(side, mix_norm[l][None, :], wmain, wsmall, wa2, ba2, bsm, SIDE_ROWS, SIDE_ROWS,
                   prefix_rows=n_meta)

        zero_state = jnp.zeros((1, 2, 2 * GLA_DV, LANES), F32)
        o_gla_meta, st_meta = _gla(sd["gq"][None], sd["gk"][None], sd["gv"][None], sd["glog"][None],
                                   zero_state, SIDE_ROWS, 1, n_valid=n_meta)
        r3 = lambda a: a.reshape(b, seq, a.shape[-1])
        o_gla, st_prompt = _gla(r3(pm["gq"]), r3(pm["gk"]), r3(pm["gv"]), r3(pm["glog"]), st_meta, cs, b)
        pad_tok = lambda a: jnp.pad(a[s0:s1][:, None, :], ((0, 0), (0, SIDE_ROWS - 1), (0, 0)))
        o_gla_s, st_sample = _gla(pad_tok(sd["gq"]), pad_tok(sd["gk"]), pad_tok(sd["gv"]), pad_tok(sd["glog"]),
                                  _state_to_pairs(state_gla[l]), SIDE_ROWS, _pick(db, (4, 2, 1)), n_valid=1)

        o_fox_meta = _fox(sd["qa"][None], sd["ka"][None], sd["va"][None], SIDE_ROWS)
        o_fox = _fox(r3(pm["qa"]), r3(pm["ka"]), r3(pm["va"]), tq, prefix=(sd["ka"], sd["va"]))

        hd3 = lambda a: a[s0:s1].reshape(db, FOX_HEADS, FOX_DH)
        q_s = sd["qa"][s0:s1].reshape(db, FOX_HEADS, LANES)[:, :, :FOX_DH]
        head_of_lane = jnp.arange(D_FOX)[None, None, :] // FOX_DH
        qbd = jnp.where(head_of_lane == jnp.arange(FOX_HEADS)[None, :, None], q_s.reshape(db, 1, D_FOX), 0)
        o_fox_s = _decode(l, page_table, qbd.astype(BF16), q_s.astype(F32), hd3(sd["fk"]),
                          sd["fv"][s0:s1][:, None, :], sd["lf"][s0:s1][:, :, None],
                          cache_kt, cache_vt, cache_lft, pages_per_step)

        tail = jnp.zeros((SIDE_ROWS - s1, D_GLA), F32)
        og_side = jnp.concatenate([o_gla_meta[0, :n_meta], o_gla_s[:, 0, :], tail], axis=0)
        of_side = jnp.concatenate([o_fox_meta[0, :n_meta], o_fox_s[:, 0, :].astype(BF16), tail.astype(BF16)], axis=0)
        gn = gla_norm[l][None, :]
        hm = _merge(hm, o_gla.reshape(b * seq, D_GLA), pm["gg"], o_fox.reshape(b * seq, D_FOX), gn, wo, tm)
        side = _merge(side, og_side, sd["gg"], of_side, gn, wo, SIDE_ROWS)
        hm = _ffn(hm, ffn2_norm[l][None, :], up2, dn2, tm)
        side = _ffn(side, ffn2_norm[l][None, :], up2, dn2, SIDE_ROWS)

        heads = lambda a: a.reshape(a.shape[:-1] + (FOX_HEADS, FOX_DH))
        with_meta = lambda m, r: jnp.concatenate(
            [jnp.broadcast_to(m[None, :n_meta], (b, n_meta) + m.shape[1:]), r.reshape((b, seq) + r.shape[1:])], axis=1)
        outs["pk"].append(heads(with_meta(sd["fk"], pm["fk"])))
        outs["pv"].append(heads(with_meta(sd["fv"], pm["fv"])))
        outs["plf"].append(with_meta(sd["lf"], pm["lf"]))
        outs["ps"].append(_pairs_to_state(st_prompt))
        outs["sk"].append(heads(sd["fk"][s0:s1][:, None, :]))
        outs["sv"].append(heads(sd["fv"][s0:s1][:, None, :]))
        outs["slf"].append(sd["lf"][s0:s1][:, None, :])
        outs["ss"].append(_pairs_to_state(st_sample))

    y_prompt = _norm(hm, final_norm[None, :], tm).reshape(b, seq, d)
    y_sample = _norm(side, final_norm[None, :], SIDE_ROWS)[s0:s1][:, None, :]
    st = lambda n: jnp.stack(outs[n])
    return (y_prompt, y_sample, st("pk"), st("pv"), st("plf"), st("ps"), st("sk"), st("sv"), st("slf"), st("ss"))
```

```python
import functools

import jax
import jax.numpy as jnp
from jax import lax
from jax.experimental import pallas as pl
from jax.experimental.pallas import tpu as pltpu

F32 = jnp.float32
BF16 = jnp.bfloat16

LANES = 128
SIDE_ROWS = 128
RMS_EPS = 1e-6
NEG = -1e30
GLA_HEADS, GLA_DK, GLA_DV, GLA_RANK = 4, 64, 128, 16
GLA_GATE_NORM = 16.0
FOX_HEADS, FOX_DH = 8, 64
D_GLA = GLA_HEADS * GLA_DV
D_FOX = FOX_HEADS * FOX_DH
GLA_QK = GLA_HEADS * GLA_DK
VMEM_LIMIT = 56 * 1024 * 1024

NT_DIMS = (((1,), (1,)), ((), ()))
TN_DIMS = (((0,), (0,)), ((), ()))


def _cparams(sem):
    return pltpu.CompilerParams(dimension_semantics=sem, vmem_limit_bytes=VMEM_LIMIT)


def _resident(shape):
    nd = len(shape)
    return pl.BlockSpec(shape, lambda *_: (0,) * nd, pipeline_mode=pl.Buffered(1))


def _rms(x, w):
    return x * lax.rsqrt(jnp.mean(x * x, axis=-1, keepdims=True) + RMS_EPS) * w


def _logsig(x):
    return jnp.minimum(x, 0.0) - jnp.log1p(jnp.exp(-jnp.abs(x)))


def _split3(x):
    hi = x.astype(BF16)
    r1 = x - hi.astype(F32)
    mid = r1.astype(BF16)
    lo = (r1 - mid.astype(F32)).astype(BF16)
    return hi, mid, lo


def _dot01_right(x, m01):
    hi, mid, lo = _split3(x)
    d = lambda a: jnp.dot(a, m01, preferred_element_type=F32)
    return (d(lo) + d(mid)) + d(hi)


def _dot01_left(m01, x):
    hi, mid, lo = _split3(x)
    d = lambda a: jnp.dot(m01, a, preferred_element_type=F32)
    return (d(lo) + d(mid)) + d(hi)


def _ffn_chunk(ff):
    return 256 if ff % 256 == 0 else LANES


def _half_ffn(x, nw, wup_ref, wdn_ref):
    ff = wdn_ref.shape[0]
    ck = _ffn_chunk(ff)
    xn = _rms(x, nw).astype(BF16)
    acc = jnp.zeros(x.shape, F32)
    for c in range(ff // ck):
        g = jnp.dot(xn, wup_ref[:, c * ck:(c + 1) * ck], preferred_element_type=F32)
        u = jnp.dot(xn, wup_ref[:, ff + c * ck:ff + (c + 1) * ck], preferred_element_type=F32)
        hm = (g * jax.nn.sigmoid(g) * u).astype(BF16)
        acc = acc + jnp.dot(hm, wdn_ref[c * ck:(c + 1) * ck, :], preferred_element_type=F32)
    return x + 0.5 * acc


def _proj_body(h_ref, fnw_ref, wup_ref, wdn_ref, nw_ref, wmain_ref, wsmall_ref, wa2_ref, ba2_ref, bsm_ref,
               tri_ref, place_ref,
               ho_ref, gq_ref, gk_ref, gv_ref, gg_ref, glog_ref, qa_ref, ka_ref, va_ref,
               fk_ref, fv_ref, lf_ref, carry_ref, *, tiles_per_seq, prefix_rows):
    @pl.when(pl.program_id(0) % tiles_per_seq == 0)
    def _():
        carry_ref[...] = jnp.zeros_like(carry_ref)

    x = _half_ffn(h_ref[...], fnw_ref[...], wup_ref, wdn_ref)
    ho_ref[...] = x
    tm = x.shape[0]
    xn = _rms(x, nw_ref[...]).astype(BF16)

    def seg(i):
        return jnp.dot(xn, wmain_ref[:, i * 512:(i + 1) * 512], preferred_element_type=F32)

    z = seg(0)
    gq_ref[...] = z[:, :GLA_QK] * (GLA_DK ** -0.5)
    gk_ref[...] = z[:, GLA_QK:]
    gv_ref[...] = seg(1).astype(BF16)
    gg_ref[...] = seg(2)

    zs = jnp.dot(xn, wsmall_ref[...], preferred_element_type=F32)
    lfull = _logsig(zs + bsm_ref[...])
    lf_ref[...] = lfull[:, :FOX_HEADS]
    x2 = jnp.dot(zs.astype(BF16), wa2_ref[...], preferred_element_type=F32) + ba2_ref[...]
    glog_ref[...] = _logsig(x2) * (1.0 / GLA_GATE_NORM)

    c = _dot01_left(tri_ref[...], lfull) + carry_ref[0:1, :]
    carry_ref[...] = jnp.broadcast_to(c[tm - 1:tm, :], carry_ref.shape)
    if prefix_rows is None:
        bias = -c
    else:
        rows = lax.broadcasted_iota(jnp.int32, (tm, 1), 0)
        bias = jnp.where(rows < prefix_rows, c[prefix_rows - 1:prefix_rows, :] - c, NEG)
    b_hi, b_mid, b_lo = _split3(bias)
    place = lambda part, i: jnp.dot(part, place_ref[i], preferred_element_type=F32)
    baug = place(b_hi, 0) + place(b_mid, 1) + place(b_lo, 2)

    lane = lax.broadcasted_iota(jnp.int32, (1, LANES), 1)
    data = lane < FOX_DH
    q_aug = jnp.where(lane < FOX_DH + 3, 1.0, 0.0)
    zq = seg(3) * (FOX_DH ** -0.5)
    zk = seg(4)
    zv = seg(5)
    fk_ref[...] = zk
    fv_ref[...] = zv
    for p in range(FOX_HEADS // 2):
        sl = slice(p * LANES, (p + 1) * LANES)
        for hh in range(2):
            h = 2 * p + hh
            hs = slice(h * LANES, (h + 1) * LANES)
            pick = (lambda t: t) if hh == 0 else (lambda t: pltpu.roll(t, FOX_DH, 1))
            qa_ref[:, hs] = jnp.where(data, pick(zq[:, sl]), q_aug).astype(BF16)
            ka_ref[:, hs] = jnp.where(data, pick(zk[:, sl]), baug[:, hs]).astype(BF16)
            va_ref[:, hs] = jnp.where(data, pick(zv[:, sl]), 1.0).astype(BF16)


def _ffn_proj(h, fnw, wup, wdn, nw, wmain, wsmall, wa2, ba2, bsm, tm, rows_per_seq, prefix_rows=None):
    rows, d = h.shape
    tri = (lax.broadcasted_iota(jnp.int32, (tm, tm), 0)
           >= lax.broadcasted_iota(jnp.int32, (tm, tm), 1)).astype(BF16)
    src_row = lax.broadcasted_iota(jnp.int32, (3, LANES, FOX_HEADS * LANES), 1)
    dst = lax.broadcasted_iota(jnp.int32, (3, LANES, FOX_HEADS * LANES), 2)
    part = lax.broadcasted_iota(jnp.int32, (3, LANES, FOX_HEADS * LANES), 0)
    place = ((dst == src_row * LANES + FOX_DH + part) & (src_row < FOX_HEADS)).astype(BF16)
    row_spec = lambda w: pl.BlockSpec((tm, w), lambda i: (i, 0))
    sds = lambda w, dt: jax.ShapeDtypeStruct((rows, w), dt)
    aug = FOX_HEADS * LANES
    outs = pl.pallas_call(
        functools.partial(_proj_body, tiles_per_seq=rows_per_seq // tm, prefix_rows=prefix_rows),
        grid=(rows // tm,),
        in_specs=[row_spec(d), _resident((1, d)), _resident(wup.shape), _resident(wdn.shape),
                  _resident((1, d)), _resident(wmain.shape), _resident(wsmall.shape),
                  _resident(wa2.shape), _resident(ba2.shape), _resident(bsm.shape), _resident(tri.shape),
                  _resident(place.shape)],
        out_specs=[row_spec(d), row_spec(GLA_QK), row_spec(GLA_QK), row_spec(D_GLA), row_spec(D_GLA), row_spec(GLA_QK),
                   row_spec(aug), row_spec(aug), row_spec(aug), row_spec(D_FOX), row_spec(D_FOX),
                   row_spec(FOX_HEADS)],
        out_shape=[sds(d, F32), sds(GLA_QK, F32), sds(GLA_QK, F32), sds(D_GLA, BF16), sds(D_GLA, F32),
                   sds(GLA_QK, F32), sds(aug, BF16), sds(aug, BF16), sds(aug, BF16), sds(D_FOX, F32), sds(D_FOX, F32),
                   sds(FOX_HEADS, F32)],
        scratch_shapes=[pltpu.VMEM((8, LANES), F32)],
        compiler_params=_cparams(("arbitrary",)),
        name="ffn_proj",
    )(h, fnw, wup, wdn, nw, wmain, wsmall, wa2, ba2, bsm, tri, place)
    keys = ("h", "gq", "gk", "gv", "gg", "glog", "qa", "ka", "va", "fk", "fv", "lf")
    return dict(zip(keys, outs))


def _gla_body(q_ref, k_ref, v_ref, lg_ref, s0_ref, o_ref, st_ref, st_scr, *, n_valid):
    c = pl.program_id(1)

    @pl.when(c == 0)
    def _():
        st_scr[...] = jnp.broadcast_to(s0_ref[...], st_scr.shape)

    bb, cs = q_ref.shape[0], q_ref.shape[1]
    row = lax.broadcasted_iota(jnp.int32, (cs, cs), 0)
    col = lax.broadcasted_iota(jnp.int32, (cs, cs), 1)
    tri = row >= col
    tri_b = tri.astype(BF16)
    lane = lax.broadcasted_iota(jnp.int32, (1, LANES), 1)
    bd = ((lax.broadcasted_iota(jnp.int32, (2 * GLA_DV, LANES), 0) < GLA_DV)
          == (lax.broadcasted_iota(jnp.int32, (2 * GLA_DV, LANES), 1) < GLA_DK))
    if n_valid is not None:
        valid = lax.broadcasted_iota(jnp.int32, (cs, 1), 0) < n_valid

    for b in range(bb):
        for p in range(2):
            ks = slice(p * LANES, (p + 1) * LANES)
            vs = slice(p * 2 * GLA_DV, (p + 1) * 2 * GLA_DV)
            q = q_ref[b, :, ks]
            k = k_ref[b, :, ks]
            lg = lg_ref[b, :, ks]
            v = v_ref[b, :, vs]
            if n_valid is not None:
                k = jnp.where(valid, k, 0.0)
                lg = jnp.where(valid, lg, 0.0)

            g = _dot01_left(tri_b, lg)
            g_last = g[cs - 1:cs, :]
            g_mid = g[cs // 2 - 1:cs // 2, :]
            qg = (q * jnp.exp(g - g_mid)).astype(BF16)
            kg = (k * jnp.exp(g_mid - g)).astype(BF16)
            kd = (k * jnp.exp(g_last - g)).astype(BF16)
            qi = (q * jnp.exp(g)).astype(BF16)

            st = st_scr[b, p]
            o_inter = lax.dot_general(qi, st.astype(BF16), NT_DIMS, preferred_element_type=F32)
            for hh in range(2):
                head = (lane < GLA_DK) if hh == 0 else (lane >= GLA_DK)
                a = lax.dot_general(jnp.where(head, qg, jnp.zeros_like(qg)), kg, NT_DIMS,
                                    preferred_element_type=F32)
                a = jnp.where(tri, a, 0.0).astype(BF16)
                hv = slice(hh * GLA_DV, (hh + 1) * GLA_DV)
                o_intra = jnp.dot(a, v[:, hv], preferred_element_type=F32)
                o_ref[b, :, p * 2 * GLA_DV + hh * GLA_DV:p * 2 * GLA_DV + (hh + 1) * GLA_DV] = o_inter[:, hv] + o_intra

            upd = lax.dot_general(v, kd, TN_DIMS, preferred_element_type=F32)
            st_new = st * jnp.exp(g_last) + jnp.where(bd, upd, 0.0)
            st_scr[b, p] = st_new

    @pl.when(c == pl.num_programs(1) - 1)
    def _():
        st_ref[...] = st_scr[...]


def _gla(q, k, v, lg, s0t, cs, bb, n_valid=None):
    bg, t, _ = q.shape
    qk_spec = pl.BlockSpec((bb, cs, GLA_QK), lambda i, c: (i, c, 0))
    v_spec = pl.BlockSpec((bb, cs, D_GLA), lambda i, c: (i, c, 0))
    st_shape = (bb, 2, 2 * GLA_DV, LANES)
    if s0t.shape[0] == 1:
        s0_spec = pl.BlockSpec((1,) + st_shape[1:], lambda i, c: (0, 0, 0, 0))
    else:
        s0_spec = pl.BlockSpec(st_shape, lambda i, c: (i, 0, 0, 0))
    return pl.pallas_call(
        functools.partial(_gla_body, n_valid=n_valid),
        grid=(bg // bb, t // cs),
        in_specs=[qk_spec, qk_spec, v_spec, qk_spec, s0_spec],
        out_specs=[v_spec, pl.BlockSpec(st_shape, lambda i, c: (i, 0, 0, 0))],
        out_shape=[jax.ShapeDtypeStruct((bg, t, D_GLA), F32),
                   jax.ShapeDtypeStruct((bg, 2, 2 * GLA_DV, LANES), F32)],
        scratch_shapes=[pltpu.VMEM(st_shape, F32)],
        compiler_params=_cparams(("parallel", "arbitrary")),
        name="gla",
    )(q, k, v, lg, s0t)


def _state_to_pairs(s):
    bs = s.shape[0]
    st = jnp.swapaxes(s, -1, -2).reshape(bs, 2, 2, GLA_DV, GLA_DK)
    z = jnp.zeros_like(st[:, :, 0])
    top = jnp.concatenate([st[:, :, 0], z], axis=-1)
    bot = jnp.concatenate([z, st[:, :, 1]], axis=-1)
    return jnp.concatenate([top, bot], axis=-2)


def _pairs_to_state(st):
    a = st[:, :, :GLA_DV, :GLA_DK]
    b = st[:, :, GLA_DV:, GLA_DK:]
    s = jnp.stack([a, b], axis=2)
    return jnp.swapaxes(s, -1, -2).reshape(st.shape[0], GLA_HEADS, GLA_DK, GLA_DV)


def _fox_body(*refs, has_prefix):
    if has_prefix:
        q_ref, k_ref, v_ref, km_ref, vm_ref, o_ref, m_scr, acc_scr = refs
    else:
        q_ref, k_ref, v_ref, o_ref, m_scr, acc_scr = refs
    i = pl.program_id(2)
    tq = q_ref.shape[0]
    m_scr[...] = jnp.full(m_scr.shape, NEG, F32)
    acc_scr[...] = jnp.zeros(acc_scr.shape, F32)

    def tile(kfn, vfn, mask, rows=slice(None)):
        for hh in range(2):
            hs = slice(hh * LANES, (hh + 1) * LANES)
            s = lax.dot_general(q_ref[rows, hs], kfn(hs), NT_DIMS, preferred_element_type=F32)
            if mask is not None:
                s = jnp.where(mask, s, NEG)
            nk = s.shape[1]
            mx = s[:, :LANES]
            for t in range(1, nk // LANES):
                mx = jnp.maximum(mx, s[:, t * LANES:(t + 1) * LANES])
            m_old = m_scr[hh, rows]
            m_new = jnp.maximum(m_old, jnp.max(mx, axis=-1, keepdims=True))
            p = jnp.exp(s - jnp.tile(m_new, (1, nk // LANES)))
            alpha = jnp.exp(m_old - m_new)
            acc_scr[hh, rows] = alpha * acc_scr[hh, rows] + jnp.dot(p.astype(BF16), vfn(hs),
                                                                     preferred_element_type=F32)
            m_scr[hh, rows] = m_new

    if has_prefix:
        tile(lambda hs: km_ref[:, hs], lambda hs: vm_ref[:, hs], None)

    def body(j, carry):
        off = pl.multiple_of(j * tq, tq)
        tile(lambda hs: k_ref[pl.ds(off, tq), hs], lambda hs: v_ref[pl.ds(off, tq), hs], None)
        return carry

    lax.fori_loop(0, i, body, 0)
    half = tq // 2 if tq % (2 * LANES) == 0 else tq
    off = pl.multiple_of(i * tq, tq)
    causal = (lax.broadcasted_iota(jnp.int32, (tq, half), 0) >= lax.broadcasted_iota(jnp.int32, (tq, half), 1))
    tile(lambda hs: k_ref[pl.ds(off, half), hs], lambda hs: v_ref[pl.ds(off, half), hs], causal)
    if half < tq:
        off2 = pl.multiple_of(i * tq + half, half)
        tile(lambda hs: k_ref[pl.ds(off2, half), hs], lambda hs: v_ref[pl.ds(off2, half), hs],
             causal[:half], rows=slice(half, tq))

    outs = []
    for hh in range(2):
        acc = acc_scr[hh]
        outs.append(acc / pltpu.roll(acc, FOX_DH, 1))
    first = lax.broadcasted_iota(jnp.int32, (1, LANES), 1) < FOX_DH
    o_ref[...] = jnp.where(first, outs[0], pltpu.roll(outs[1], FOX_DH, 1)).astype(BF16)


def _fox(qa, ka, va, tq, prefix=None):
    b, l, _ = qa.shape
    q_spec = pl.BlockSpec((None, tq, 2 * LANES), lambda bb, p, i: (bb, i, p))
    kv_spec = pl.BlockSpec((None, l, 2 * LANES), lambda bb, p, i: (bb, 0, p))
    in_specs = [q_spec, kv_spec, kv_spec]
    args = [qa, ka, va]
    if prefix is not None:
        pk_spec = pl.BlockSpec((SIDE_ROWS, 2 * LANES), lambda bb, p, i: (0, p))
        in_specs += [pk_spec, pk_spec]
        args += list(prefix)
    return pl.pallas_call(
        functools.partial(_fox_body, has_prefix=prefix is not None),
        grid=(b, FOX_HEADS // 2, l // tq),
        in_specs=in_specs,
        out_specs=pl.BlockSpec((None, tq, LANES), lambda bb, p, i: (bb, i, p)),
        out_shape=jax.ShapeDtypeStruct((b, l, D_FOX), BF16),
        scratch_shapes=[pltpu.VMEM((2, tq, LANES), F32), pltpu.VMEM((2, tq, LANES), F32)],
        compiler_params=_cparams(("parallel", "parallel", "arbitrary")),
        name="fox",
    )(*args)


def _decode_body(pt_ref, qbd_ref, q_ref, knew_ref, vnew_ref, lfnew_ref, u_ref, *rest, n_pages):
    del pt_ref
    k_refs = rest[:n_pages]
    v_refs = rest[n_pages:2 * n_pages]
    lf_refs = rest[2 * n_pages:3 * n_pages]
    o_ref, m_scr, l_scr, acc_scr, carry_scr = rest[3 * n_pages:]
    j = pl.program_id(1)
    qbd = qbd_ref[...]
    width = qbd.shape[1]
    page = u_ref.shape[0]
    head_lane = (lax.broadcasted_iota(jnp.int32, (FOX_HEADS, width), 1) // FOX_DH
                 == lax.broadcasted_iota(jnp.int32, (FOX_HEADS, width), 0))

    @pl.when(j == 0)
    def _():
        s_new = jnp.sum(q_ref[...] * knew_ref[...], axis=-1, keepdims=True)
        m_scr[...] = jnp.broadcast_to(s_new, m_scr.shape)
        l_scr[...] = jnp.ones(l_scr.shape, F32)
        acc_scr[...] = jnp.where(head_lane, jnp.broadcast_to(vnew_ref[...], acc_scr.shape), 0.0)
        carry_scr[...] = jnp.zeros(carry_scr.shape, F32)

    lf_new = lfnew_ref[...]
    carry = carry_scr[...] + lf_new
    scores = []
    for g in reversed(range(n_pages)):
        lft = lf_refs[g][...]
        suffix = _dot01_right(lft, u_ref[...])
        kt = k_refs[g][...].reshape(width, page).astype(BF16)
        scores.append(jnp.dot(qbd, kt, preferred_element_type=F32) + (suffix + carry))
        carry = carry + jnp.sum(lft, axis=-1, keepdims=True)
    carry_scr[...] = carry - lf_new

    mx = scores[0]
    for s in scores[1:]:
        mx = jnp.maximum(mx, s)
    m_old = m_scr[...]
    m_new = jnp.maximum(m_old, jnp.max(mx, axis=-1, keepdims=True))
    alpha = jnp.exp(m_old - m_new)
    psum = jnp.zeros(m_old.shape, F32)
    pv = jnp.zeros(acc_scr.shape, F32)
    for s, g in zip(scores, reversed(range(n_pages))):
        p = jnp.exp(s - m_new)
        psum = psum + p
        vt = v_refs[g][...].reshape(width, page).astype(BF16)
        pv = pv + lax.dot_general(p.astype(BF16), vt, NT_DIMS, preferred_element_type=F32)
    l_scr[...] = alpha * l_scr[...] + jnp.sum(psum, axis=-1, keepdims=True)
    acc_scr[...] = alpha[:, 0:1] * acc_scr[...] + pv
    m_scr[...] = m_new

    @pl.when(j == pl.num_programs(1) - 1)
    def _():
        o = jnp.where(head_lane, acc_scr[...] / l_scr[:, 0:1], 0.0)
        o_ref[...] = jnp.sum(o, axis=0, keepdims=True)


def _decode(layer, page_table, qbd, q, knew, vnew, lfnew, cache_kt, cache_vt, cache_lft, n_pages):
    db, total_pages = page_table.shape
    page = cache_kt.shape[-1]
    groups = total_pages // n_pages
    strict_upper = (lax.broadcasted_iota(jnp.int32, (page, page), 0)
                    > lax.broadcasted_iota(jnp.int32, (page, page), 1)).astype(BF16)

    def page_map(nd):
        def make(g):
            return lambda b, j, pt: (layer, pt[b, (groups - 1 - j) * n_pages + g]) + (0,) * nd
        return make

    seq3 = lambda b, j, pt: (b, 0, 0)
    kv_specs = [pl.BlockSpec((None, None, FOX_HEADS, FOX_DH, page), page_map(3)(g)) for g in range(n_pages)]
    lf_specs = [pl.BlockSpec((None, None, FOX_HEADS, page), page_map(2)(g)) for g in range(n_pages)]
    head_spec = pl.BlockSpec((None, FOX_HEADS, FOX_DH), seq3)
    flat_spec = pl.BlockSpec((None, 1, D_FOX), seq3)
    grid_spec = pltpu.PrefetchScalarGridSpec(
        num_scalar_prefetch=1,
        grid=(db, groups),
        in_specs=[pl.BlockSpec((None, FOX_HEADS, D_FOX), seq3), head_spec, head_spec, flat_spec,
                  pl.BlockSpec((None, FOX_HEADS, 1), seq3), pl.BlockSpec((page, page), lambda b, j, pt: (0, 0))]
                 + kv_specs + kv_specs + lf_specs,
        out_specs=flat_spec,
        scratch_shapes=[pltpu.VMEM((FOX_HEADS, LANES), F32), pltpu.VMEM((FOX_HEADS, LANES), F32),
                        pltpu.VMEM((FOX_HEADS, D_FOX), F32), pltpu.VMEM((FOX_HEADS, LANES), F32)],
    )
    return pl.pallas_call(
        functools.partial(_decode_body, n_pages=n_pages),
        grid_spec=grid_spec,
        out_shape=jax.ShapeDtypeStruct((db, 1, D_FOX), F32),
        compiler_params=_cparams(("parallel", "arbitrary")),
        name="fox_decode",
    )(page_table, qbd, q, knew, vnew, lfnew, strict_upper,
      *([cache_kt] * n_pages), *([cache_vt] * n_pages), *([cache_lft] * n_pages))


def _merge_ffn_body(h_ref, og_ref, gg_ref, of_ref, gn_ref, wo_ref, nw_ref, wup_ref, wdn_ref, fn_ref, o_ref,
                    *, final):
    og = og_ref[...]
    gg = gg_ref[...]
    gn = gn_ref[...]
    parts = []
    for hd in range(GLA_HEADS):
        oh = og[:, hd * GLA_DV:(hd + 1) * GLA_DV]
        on = oh * lax.rsqrt(jnp.mean(oh * oh, axis=-1, keepdims=True) + RMS_EPS) * gn
        gh = gg[:, hd * GLA_DV:(hd + 1) * GLA_DV]
        parts.append((on * (gh * jax.nn.sigmoid(gh))).astype(BF16))
    gla_part = jnp.concatenate(parts, axis=-1)
    mixed = (jnp.dot(gla_part, wo_ref[:D_GLA, :], preferred_element_type=F32)
             + jnp.dot(of_ref[...], wo_ref[D_GLA:, :], preferred_element_type=F32))
    x = h_ref[...] + mixed

    out = _half_ffn(x, nw_ref[...], wup_ref, wdn_ref)
    o_ref[...] = _rms(out, fn_ref[...]) if final else out


def _merge_ffn(h, o_gla, gg, o_fox, gn, wo, nw, wup, wdn, fn, tm, final):
    rows, d = h.shape
    row_spec = lambda w: pl.BlockSpec((tm, w), lambda i: (i, 0))
    return pl.pallas_call(
        functools.partial(_merge_ffn_body, final=final),
        grid=(rows // tm,),
        in_specs=[row_spec(d), row_spec(D_GLA), row_spec(D_GLA), row_spec(D_FOX),
                  _resident(gn.shape), _resident(wo.shape), _resident((1, d)), _resident(wup.shape),
                  _resident(wdn.shape), _resident((1, d))],
        out_specs=row_spec(d),
        out_shape=jax.ShapeDtypeStruct((rows, d), F32),
        compiler_params=_cparams(("parallel",)),
        name="merge_ffn",
    )(h, o_gla, gg, o_fox, gn, wo, nw, wup, wdn, fn)


def _pick(n, prefs):
    for t in prefs:
        if n % t == 0:
            return t
    raise ValueError(f"no tile in {prefs} divides {n}")


def _pack_w_in(w_in_l, b_f_l, w_a2_l, b_a2_l):
    sizes = (GLA_QK, GLA_QK, D_GLA, D_GLA, GLA_RANK, D_FOX, D_FOX, D_FOX, FOX_HEADS)
    offs = [0]
    for s in sizes:
        offs.append(offs[-1] + s)
    col = lambda i: w_in_l[:, offs[i]:offs[i + 1]]
    wmain = jnp.concatenate([col(0), col(1), col(2), col(3), col(5), col(6), col(7)], axis=1).astype(BF16)
    d = w_in_l.shape[0]
    pad = LANES - FOX_HEADS - GLA_RANK
    wsmall = jnp.concatenate([col(8), col(4), jnp.zeros((d, pad), F32)], axis=1).astype(BF16)
    bsm = jnp.concatenate([b_f_l, jnp.zeros((LANES - FOX_HEADS,), F32)])[None, :]
    wa2 = jnp.concatenate([jnp.zeros((FOX_HEADS, GLA_QK), F32), w_a2_l, jnp.zeros((pad, GLA_QK), F32)],
                          axis=0).astype(BF16)
    return wmain, wsmall, wa2, b_a2_l[None, :], bsm


def kernel(x_prompt, x_sample, cache_fox_k, cache_fox_v, cache_fox_logf, state_gla, page_table, meta_tokens,
           ffn1_norm, ffn1_w_up, ffn1_w_down, mix_norm, w_in, fox_b_f, gla_w_a2, gla_b_a2, gla_norm, w_out,
           ffn2_norm, ffn2_w_up, ffn2_w_down, final_norm):
    b, seq, d = x_prompt.shape
    db, dec_seq, _ = x_sample.shape
    n_meta = meta_tokens.shape[0]
    depth = w_in.shape[0]
    n_pool, page = cache_fox_k.shape[1], cache_fox_k.shape[2]
    assert dec_seq == 1 and n_meta + db <= SIDE_ROWS and page == LANES
    s0, s1 = n_meta, n_meta + db

    tm = _pick(seq, (512, 256, 128))
    tq = _pick(seq, (1024, 512, 256, 128))
    cs = _pick(seq, (64,))
    pages_per_step = _pick(page_table.shape[1], (16, 8, 4, 2, 1))

    hm = x_prompt.reshape(b * seq, d)
    side = jnp.concatenate([meta_tokens, x_sample[:, 0, :], jnp.zeros((SIDE_ROWS - s1, d), F32)], axis=0)

    cache_kt = jnp.transpose(cache_fox_k, (0, 1, 3, 4, 2))
    cache_vt = jnp.transpose(cache_fox_v, (0, 1, 3, 4, 2))
    cache_lft = jnp.swapaxes(cache_fox_logf, -1, -2)

    outs = {n: [] for n in ("pk", "pv", "plf", "ps", "sk", "sv", "slf", "ss")}
    for l in range(depth):
        up1, dn1 = ffn1_w_up[l].astype(BF16), ffn1_w_down[l].astype(BF16)
        up2, dn2 = ffn2_w_up[l].astype(BF16), ffn2_w_down[l].astype(BF16)
        wmain, wsmall, wa2, ba2, bsm = _pack_w_in(w_in[l], fox_b_f[l], gla_w_a2[l], gla_b_a2[l])
        wo = w_out[l].astype(BF16)

        n1, nm = ffn1_norm[l][None, :], mix_norm[l][None, :]
        pm = _ffn_proj(hm, n1, up1, dn1, nm, wmain, wsmall, wa2, ba2, bsm, tm, seq)
        sd = _ffn_proj(side, n1, up1, dn1, nm, wmain, wsmall, wa2, ba2, bsm, SIDE_ROWS, SIDE_ROWS,
                       prefix_rows=n_meta)
        hm, side = pm["h"], sd["h"]

        zero_state = jnp.zeros((1, 2, 2 * GLA_DV, LANES), F32)
        o_gla_meta, st_meta = _gla(sd["gq"][None], sd["gk"][None], sd["gv"][None], sd["glog"][None],
                                   zero_state, SIDE_ROWS, 1, n_valid=n_meta)
        r3 = lambda a: a.reshape(b, seq, a.shape[-1])
        o_gla, st_prompt = _gla(r3(pm["gq"]), r3(pm["gk"]), r3(pm["gv"]), r3(pm["glog"]), st_meta, cs, b)
        pad_tok = lambda a: jnp.pad(a[s0:s1][:, None, :], ((0, 0), (0, SIDE_ROWS - 1), (0, 0)))
        o_gla_s, st_sample = _gla(pad_tok(sd["gq"]), pad_tok(sd["gk"]), pad_tok(sd["gv"]), pad_tok(sd["glog"]),
                                  _state_to_pairs(state_gla[l]), SIDE_ROWS, _pick(db, (4, 2, 1)), n_valid=1)

        o_fox_meta = _fox(sd["qa"][None], sd["ka"][None], sd["va"][None], SIDE_ROWS)
        o_fox = _fox(r3(pm["qa"]), r3(pm["ka"]), r3(pm["va"]), tq, prefix=(sd["ka"], sd["va"]))

        hd3 = lambda a: a[s0:s1].reshape(db, FOX_HEADS, FOX_DH)
        q_s = sd["qa"][s0:s1].reshape(db, FOX_HEADS, LANES)[:, :, :FOX_DH]
        head_of_lane = jnp.arange(D_FOX)[None, None, :] // FOX_DH
        qbd = jnp.where(head_of_lane == jnp.arange(FOX_HEADS)[None, :, None], q_s.reshape(db, 1, D_FOX), 0)
        o_fox_s = _decode(l, page_table, qbd.astype(BF16), q_s.astype(F32), hd3(sd["fk"]),
                          sd["fv"][s0:s1][:, None, :], sd["lf"][s0:s1][:, :, None],
                          cache_kt, cache_vt, cache_lft, pages_per_step)

        tail = jnp.zeros((SIDE_ROWS - s1, D_GLA), F32)
        og_side = jnp.concatenate([o_gla_meta[0, :n_meta], o_gla_s[:, 0, :], tail], axis=0)
        of_side = jnp.concatenate([o_fox_meta[0, :n_meta], o_fox_s[:, 0, :].astype(BF16), tail.astype(BF16)], axis=0)
        gn = gla_norm[l][None, :]
        last = l == depth - 1
        fn = final_norm[None, :]
        n2 = ffn2_norm[l][None, :]
        hm = _merge_ffn(hm, o_gla.reshape(b * seq, D_GLA), pm["gg"], o_fox.reshape(b * seq, D_FOX), gn, wo,
                        n2, up2, dn2, fn, tm, last)
        side = _merge_ffn(side, og_side, sd["gg"], of_side, gn, wo, n2, up2, dn2, fn, SIDE_ROWS, last)

        heads = lambda a: a.reshape(a.shape[:-1] + (FOX_HEADS, FOX_DH))
        with_meta = lambda m, r: jnp.concatenate(
            [jnp.broadcast_to(m[None, :n_meta], (b, n_meta) + m.shape[1:]), r.reshape((b, seq) + r.shape[1:])], axis=1)
        outs["pk"].append(heads(with_meta(sd["fk"], pm["fk"])))
        outs["pv"].append(heads(with_meta(sd["fv"], pm["fv"])))
        outs["plf"].append(with_meta(sd["lf"], pm["lf"]))
        outs["ps"].append(_pairs_to_state(st_prompt))
        outs["sk"].append(heads(sd["fk"][s0:s1][:, None, :]))
        outs["sv"].append(heads(sd["fv"][s0:s1][:, None, :]))
        outs["slf"].append(sd["lf"][s0:s1][:, None, :])
        outs["ss"].append(_pairs_to_state(st_sample))

    y_prompt = hm.reshape(b, seq, d)
    y_sample = side[s0:s1][:, None, :]
    st = lambda n: jnp.stack(outs[n])
    return (y_prompt, y_sample, st("pk"), st("pv"), st("plf"), st("ps"), st("sk"), st("sv"), st("slf"), st("ss"))
```

```python
import functools

import jax
import jax.numpy as jnp
from jax import lax
from jax.experimental import pallas as pl
from jax.experimental.pallas import tpu as pltpu

F32 = jnp.float32
BF16 = jnp.bfloat16

LANES = 128
SIDE_ROWS = 128
RMS_EPS = 1e-6
NEG = -1e30
GLA_HEADS, GLA_DK, GLA_DV, GLA_RANK = 4, 64, 128, 16
GLA_GATE_NORM = 16.0
FOX_HEADS, FOX_DH = 8, 64
D_GLA = GLA_HEADS * GLA_DV
D_FOX = FOX_HEADS * FOX_DH
GLA_QK = GLA_HEADS * GLA_DK
VMEM_LIMIT = 56 * 1024 * 1024

NT_DIMS = (((1,), (1,)), ((), ()))
TN_DIMS = (((0,), (0,)), ((), ()))


def _cparams(sem):
    return pltpu.CompilerParams(dimension_semantics=sem, vmem_limit_bytes=VMEM_LIMIT)


def _resident(shape):
    nd = len(shape)
    return pl.BlockSpec(shape, lambda *_: (0,) * nd, pipeline_mode=pl.Buffered(1))


def _rms(x, w):
    return x * lax.rsqrt(jnp.mean(x * x, axis=-1, keepdims=True) + RMS_EPS) * w


def _logsig(x):
    return jnp.minimum(x, 0.0) - jnp.log1p(jnp.exp(-jnp.abs(x)))


def _split3(x):
    hi = x.astype(BF16)
    r1 = x - hi.astype(F32)
    mid = r1.astype(BF16)
    lo = (r1 - mid.astype(F32)).astype(BF16)
    return hi, mid, lo


def _dot01_right(x, m01):
    hi, mid, lo = _split3(x)
    d = lambda a: jnp.dot(a, m01, preferred_element_type=F32)
    return (d(lo) + d(mid)) + d(hi)


def _dot01_left(m01, x):
    hi, mid, lo = _split3(x)
    d = lambda a: jnp.dot(m01, a, preferred_element_type=F32)
    return (d(lo) + d(mid)) + d(hi)


def _ffn_chunk(ff):
    return 256 if ff % 256 == 0 else LANES


def _half_ffn(x, nw, wup_ref, wdn_ref):
    ff = wdn_ref.shape[0]
    ck = _ffn_chunk(ff)
    xn = _rms(x, nw).astype(BF16)
    acc = jnp.zeros(x.shape, F32)
    for c in range(ff // ck):
        g = jnp.dot(xn, wup_ref[:, c * ck:(c + 1) * ck], preferred_element_type=F32)
        u = jnp.dot(xn, wup_ref[:, ff + c * ck:ff + (c + 1) * ck], preferred_element_type=F32)
        hm = (g * jax.nn.sigmoid(g) * u).astype(BF16)
        acc = acc + jnp.dot(hm, wdn_ref[c * ck:(c + 1) * ck, :], preferred_element_type=F32)
    return x + 0.5 * acc


def _proj_body(h_ref, fnw_ref, wup_ref, wdn_ref, nw_ref, wmain_ref, wsmall_ref, wa2_ref, ba2_ref, bsm_ref,
               tri_ref, place_ref,
               ho_ref, gq_ref, gk_ref, gv_ref, gg_ref, glog_ref, qa_ref, ka_ref, va_ref,
               fk_ref, fv_ref, lf_ref, carry_ref, *, tiles_per_seq, prefix_rows):
    @pl.when(pl.program_id(0) % tiles_per_seq == 0)
    def _():
        carry_ref[...] = jnp.zeros_like(carry_ref)

    x = _half_ffn(h_ref[...], fnw_ref[...], wup_ref, wdn_ref)
    ho_ref[...] = x
    tm = x.shape[0]
    xn = _rms(x, nw_ref[...]).astype(BF16)

    def seg(i):
        return jnp.dot(xn, wmain_ref[:, i * 512:(i + 1) * 512], preferred_element_type=F32)

    z = seg(0)
    gq_ref[...] = z[:, :GLA_QK] * (GLA_DK ** -0.5)
    gk_ref[...] = z[:, GLA_QK:]
    gv_ref[...] = seg(1).astype(BF16)
    gg_ref[...] = seg(2)

    zs = jnp.dot(xn, wsmall_ref[...], preferred_element_type=F32)
    lfull = _logsig(zs + bsm_ref[...])
    lf_ref[...] = lfull[:, :FOX_HEADS]
    x2 = jnp.dot(zs.astype(BF16), wa2_ref[...], preferred_element_type=F32) + ba2_ref[...]
    glog_ref[...] = _logsig(x2) * (1.0 / GLA_GATE_NORM)

    c = _dot01_left(tri_ref[...], lfull) + carry_ref[0:1, :]
    carry_ref[...] = jnp.broadcast_to(c[tm - 1:tm, :], carry_ref.shape)
    if prefix_rows is None:
        bias = -c
    else:
        rows = lax.broadcasted_iota(jnp.int32, (tm, 1), 0)
        bias = jnp.where(rows < prefix_rows, c[prefix_rows - 1:prefix_rows, :] - c, NEG)
    b_hi, b_mid, b_lo = _split3(bias)
    place = lambda part, i: jnp.dot(part, place_ref[i], preferred_element_type=F32)
    baug = place(b_hi, 0) + place(b_mid, 1) + place(b_lo, 2)

    lane = lax.broadcasted_iota(jnp.int32, (1, LANES), 1)
    data = lane < FOX_DH
    q_aug = jnp.where(lane < FOX_DH + 3, 1.0, 0.0)
    zq = seg(3) * (FOX_DH ** -0.5)
    zk = seg(4)
    zv = seg(5)
    fk_ref[...] = zk
    fv_ref[...] = zv
    for p in range(FOX_HEADS // 2):
        sl = slice(p * LANES, (p + 1) * LANES)
        for hh in range(2):
            h = 2 * p + hh
            hs = slice(h * LANES, (h + 1) * LANES)
            pick = (lambda t: t) if hh == 0 else (lambda t: pltpu.roll(t, FOX_DH, 1))
            qa_ref[:, hs] = jnp.where(data, pick(zq[:, sl]), q_aug).astype(BF16)
            ka_ref[:, hs] = jnp.where(data, pick(zk[:, sl]), baug[:, hs]).astype(BF16)
            va_ref[:, hs] = jnp.where(data, pick(zv[:, sl]), 1.0).astype(BF16)


def _ffn_proj(h, fnw, wup, wdn, nw, wmain, wsmall, wa2, ba2, bsm, tm, rows_per_seq, prefix_rows=None):
    rows, d = h.shape
    tri = (lax.broadcasted_iota(jnp.int32, (tm, tm), 0)
           >= lax.broadcasted_iota(jnp.int32, (tm, tm), 1)).astype(BF16)
    src_row = lax.broadcasted_iota(jnp.int32, (3, LANES, FOX_HEADS * LANES), 1)
    dst = lax.broadcasted_iota(jnp.int32, (3, LANES, FOX_HEADS * LANES), 2)
    part = lax.broadcasted_iota(jnp.int32, (3, LANES, FOX_HEADS * LANES), 0)
    place = ((dst == src_row * LANES + FOX_DH + part) & (src_row < FOX_HEADS)).astype(BF16)
    row_spec = lambda w: pl.BlockSpec((tm, w), lambda i: (i, 0))
    sds = lambda w, dt: jax.ShapeDtypeStruct((rows, w), dt)
    aug = FOX_HEADS * LANES
    outs = pl.pallas_call(
        functools.partial(_proj_body, tiles_per_seq=rows_per_seq // tm, prefix_rows=prefix_rows),
        grid=(rows // tm,),
        in_specs=[row_spec(d), _resident((1, d)), _resident(wup.shape), _resident(wdn.shape),
                  _resident((1, d)), _resident(wmain.shape), _resident(wsmall.shape),
                  _resident(wa2.shape), _resident(ba2.shape), _resident(bsm.shape), _resident(tri.shape),
                  _resident(place.shape)],
        out_specs=[row_spec(d), row_spec(GLA_QK), row_spec(GLA_QK), row_spec(D_GLA), row_spec(D_GLA), row_spec(GLA_QK),
                   row_spec(aug), row_spec(aug), row_spec(aug), row_spec(D_FOX), row_spec(D_FOX),
                   row_spec(FOX_HEADS)],
        out_shape=[sds(d, F32), sds(GLA_QK, F32), sds(GLA_QK, F32), sds(D_GLA, BF16), sds(D_GLA, F32),
                   sds(GLA_QK, F32), sds(aug, BF16), sds(aug, BF16), sds(aug, BF16), sds(D_FOX, F32), sds(D_FOX, F32),
                   sds(FOX_HEADS, F32)],
        scratch_shapes=[pltpu.VMEM((8, LANES), F32)],
        compiler_params=_cparams(("arbitrary",)),
        name="ffn_proj",
    )(h, fnw, wup, wdn, nw, wmain, wsmall, wa2, ba2, bsm, tri, place)
    keys = ("h", "gq", "gk", "gv", "gg", "glog", "qa", "ka", "va", "fk", "fv", "lf")
    return dict(zip(keys, outs))


def _gla_body(q_ref, k_ref, v_ref, lg_ref, s0_ref, o_ref, st_ref, st_scr, *, cs, n_valid):
    c = pl.program_id(1)

    @pl.when(c == 0)
    def _():
        st_scr[...] = jnp.broadcast_to(s0_ref[...], st_scr.shape)

    bb = q_ref.shape[0]
    n_sub = q_ref.shape[1] // cs
    row = lax.broadcasted_iota(jnp.int32, (cs, cs), 0)
    col = lax.broadcasted_iota(jnp.int32, (cs, cs), 1)
    tri = row >= col
    tri_b = tri.astype(BF16)
    lane = lax.broadcasted_iota(jnp.int32, (1, LANES), 1)
    bd = ((lax.broadcasted_iota(jnp.int32, (2 * GLA_DV, LANES), 0) < GLA_DV)
          == (lax.broadcasted_iota(jnp.int32, (2 * GLA_DV, LANES), 1) < GLA_DK))
    if n_valid is not None:
        valid = lax.broadcasted_iota(jnp.int32, (cs, 1), 0) < n_valid

    for b in range(bb):
        for p in range(2):
            ks = slice(p * LANES, (p + 1) * LANES)
            vs = slice(p * 2 * GLA_DV, (p + 1) * 2 * GLA_DV)
            st = st_scr[b, p]
            for sub in range(n_sub):
                rs = slice(sub * cs, (sub + 1) * cs)
                q = q_ref[b, rs, ks]
                k = k_ref[b, rs, ks]
                lg = lg_ref[b, rs, ks]
                v = v_ref[b, rs, vs]
                if n_valid is not None:
                    k = jnp.where(valid, k, 0.0)
                    lg = jnp.where(valid, lg, 0.0)

                g = _dot01_left(tri_b, lg)
                g_last = g[cs - 1:cs, :]
                g_mid = g[cs // 2 - 1:cs // 2, :]
                qg = (q * jnp.exp(g - g_mid)).astype(BF16)
                kg = (k * jnp.exp(g_mid - g)).astype(BF16)
                kd = (k * jnp.exp(g_last - g)).astype(BF16)
                qi = (q * jnp.exp(g)).astype(BF16)

                o_inter = lax.dot_general(qi, st.astype(BF16), NT_DIMS, preferred_element_type=F32)
                for hh in range(2):
                    head = (lane < GLA_DK) if hh == 0 else (lane >= GLA_DK)
                    a = lax.dot_general(jnp.where(head, qg, jnp.zeros_like(qg)), kg, NT_DIMS,
                                        preferred_element_type=F32)
                    a = jnp.where(tri, a, 0.0).astype(BF16)
                    hv = slice(hh * GLA_DV, (hh + 1) * GLA_DV)
                    o_intra = jnp.dot(a, v[:, hv], preferred_element_type=F32)
                    o_ref[b, rs, p * 2 * GLA_DV + hh * GLA_DV:p * 2 * GLA_DV + (hh + 1) * GLA_DV] = (
                        o_inter[:, hv] + o_intra)

                upd = lax.dot_general(v, kd, TN_DIMS, preferred_element_type=F32)
                st = st * jnp.exp(g_last) + jnp.where(bd, upd, 0.0)
            st_scr[b, p] = st

    @pl.when(c == pl.num_programs(1) - 1)
    def _():
        st_ref[...] = st_scr[...]


def _gla(q, k, v, lg, s0t, cs, bb, n_valid=None, n_sub=1):
    bg, t, _ = q.shape
    blk = cs * n_sub
    qk_spec = pl.BlockSpec((bb, blk, GLA_QK), lambda i, c: (i, c, 0))
    v_spec = pl.BlockSpec((bb, blk, D_GLA), lambda i, c: (i, c, 0))
    st_shape = (bb, 2, 2 * GLA_DV, LANES)
    if s0t.shape[0] == 1:
        s0_spec = pl.BlockSpec((1,) + st_shape[1:], lambda i, c: (0, 0, 0, 0))
    else:
        s0_spec = pl.BlockSpec(st_shape, lambda i, c: (i, 0, 0, 0))
    return pl.pallas_call(
        functools.partial(_gla_body, cs=cs, n_valid=n_valid),
        grid=(bg // bb, t // blk),
        in_specs=[qk_spec, qk_spec, v_spec, qk_spec, s0_spec],
        out_specs=[v_spec, pl.BlockSpec(st_shape, lambda i, c: (i, 0, 0, 0))],
        out_shape=[jax.ShapeDtypeStruct((bg, t, D_GLA), F32),
                   jax.ShapeDtypeStruct((bg, 2, 2 * GLA_DV, LANES), F32)],
        scratch_shapes=[pltpu.VMEM(st_shape, F32)],
        compiler_params=_cparams(("parallel", "arbitrary")),
        name="gla",
    )(q, k, v, lg, s0t)


def _state_to_pairs(s):
    bs = s.shape[0]
    st = jnp.swapaxes(s, -1, -2).reshape(bs, 2, 2, GLA_DV, GLA_DK)
    z = jnp.zeros_like(st[:, :, 0])
    top = jnp.concatenate([st[:, :, 0], z], axis=-1)
    bot = jnp.concatenate([z, st[:, :, 1]], axis=-1)
    return jnp.concatenate([top, bot], axis=-2)


def _pairs_to_state(st):
    a = st[:, :, :GLA_DV, :GLA_DK]
    b = st[:, :, GLA_DV:, GLA_DK:]
    s = jnp.stack([a, b], axis=2)
    return jnp.swapaxes(s, -1, -2).reshape(st.shape[0], GLA_HEADS, GLA_DK, GLA_DV)


def _fox_body(*refs, has_prefix):
    if has_prefix:
        q_ref, k_ref, v_ref, km_ref, vm_ref, o_ref, m_scr, acc_scr = refs
    else:
        q_ref, k_ref, v_ref, o_ref, m_scr, acc_scr = refs
    i = pl.program_id(2)
    tq = q_ref.shape[0]
    m_scr[...] = jnp.full(m_scr.shape, NEG, F32)
    acc_scr[...] = jnp.zeros(acc_scr.shape, F32)

    def tile(kfn, vfn, mask, rows=slice(None)):
        for hh in range(2):
            hs = slice(hh * LANES, (hh + 1) * LANES)
            s = lax.dot_general(q_ref[rows, hs], kfn(hs), NT_DIMS, preferred_element_type=F32)
            if mask is not None:
                s = jnp.where(mask, s, NEG)
            nk = s.shape[1]
            mx = s[:, :LANES]
            for t in range(1, nk // LANES):
                mx = jnp.maximum(mx, s[:, t * LANES:(t + 1) * LANES])
            m_old = m_scr[hh, rows]
            m_new = jnp.maximum(m_old, jnp.max(mx, axis=-1, keepdims=True))
            p = jnp.exp(s - jnp.tile(m_new, (1, nk // LANES)))
            alpha = jnp.exp(m_old - m_new)
            acc_scr[hh, rows] = alpha * acc_scr[hh, rows] + jnp.dot(p.astype(BF16), vfn(hs),
                                                                     preferred_element_type=F32)
            m_scr[hh, rows] = m_new

    if has_prefix:
        tile(lambda hs: km_ref[:, hs], lambda hs: vm_ref[:, hs], None)

    def body(j, carry):
        off = pl.multiple_of(j * tq, tq)
        tile(lambda hs: k_ref[pl.ds(off, tq), hs], lambda hs: v_ref[pl.ds(off, tq), hs], None)
        return carry

    lax.fori_loop(0, i, body, 0)
    half = tq // 2 if tq % (2 * LANES) == 0 else tq
    off = pl.multiple_of(i * tq, tq)
    causal = (lax.broadcasted_iota(jnp.int32, (tq, half), 0) >= lax.broadcasted_iota(jnp.int32, (tq, half), 1))
    tile(lambda hs: k_ref[pl.ds(off, half), hs], lambda hs: v_ref[pl.ds(off, half), hs], causal)
    if half < tq:
        off2 = pl.multiple_of(i * tq + half, half)
        tile(lambda hs: k_ref[pl.ds(off2, half), hs], lambda hs: v_ref[pl.ds(off2, half), hs],
             causal[:half], rows=slice(half, tq))

    outs = []
    for hh in range(2):
        acc = acc_scr[hh]
        outs.append(acc / pltpu.roll(acc, FOX_DH, 1))
    first = lax.broadcasted_iota(jnp.int32, (1, LANES), 1) < FOX_DH
    o_ref[...] = jnp.where(first, outs[0], pltpu.roll(outs[1], FOX_DH, 1)).astype(BF16)


def _fox(qa, ka, va, tq, prefix=None):
    b, l, _ = qa.shape
    q_spec = pl.BlockSpec((None, tq, 2 * LANES), lambda bb, p, i: (bb, i, p))
    kv_spec = pl.BlockSpec((None, l, 2 * LANES), lambda bb, p, i: (bb, 0, p))
    in_specs = [q_spec, kv_spec, kv_spec]
    args = [qa, ka, va]
    if prefix is not None:
        pk_spec = pl.BlockSpec((SIDE_ROWS, 2 * LANES), lambda bb, p, i: (0, p))
        in_specs += [pk_spec, pk_spec]
        args += list(prefix)
    return pl.pallas_call(
        functools.partial(_fox_body, has_prefix=prefix is not None),
        grid=(b, FOX_HEADS // 2, l // tq),
        in_specs=in_specs,
        out_specs=pl.BlockSpec((None, tq, LANES), lambda bb, p, i: (bb, i, p)),
        out_shape=jax.ShapeDtypeStruct((b, l, D_FOX), BF16),
        scratch_shapes=[pltpu.VMEM((2, tq, LANES), F32), pltpu.VMEM((2, tq, LANES), F32)],
        compiler_params=_cparams(("parallel", "parallel", "arbitrary")),
        name="fox",
    )(*args)


def _decode_body(pt_ref, qbd_ref, q_ref, knew_ref, vnew_ref, lfnew_ref, u_ref, *rest, n_pages):
    del pt_ref
    k_refs = rest[:n_pages]
    v_refs = rest[n_pages:2 * n_pages]
    lf_refs = rest[2 * n_pages:3 * n_pages]
    o_ref, m_scr, l_scr, acc_scr, carry_scr = rest[3 * n_pages:]
    j = pl.program_id(1)
    qbd = qbd_ref[...]
    width = qbd.shape[1]
    page = u_ref.shape[0]
    head_lane = (lax.broadcasted_iota(jnp.int32, (FOX_HEADS, width), 1) // FOX_DH
                 == lax.broadcasted_iota(jnp.int32, (FOX_HEADS, width), 0))

    @pl.when(j == 0)
    def _():
        s_new = jnp.sum(q_ref[...] * knew_ref[...], axis=-1, keepdims=True)
        m_scr[...] = jnp.broadcast_to(s_new, m_scr.shape)
        l_scr[...] = jnp.ones(l_scr.shape, F32)
        acc_scr[...] = jnp.where(head_lane, jnp.broadcast_to(vnew_ref[...], acc_scr.shape), 0.0)
        carry_scr[...] = jnp.zeros(carry_scr.shape, F32)

    lf_new = lfnew_ref[...]
    carry = carry_scr[...] + lf_new
    scores = []
    for g in reversed(range(n_pages)):
        lft = lf_refs[g][...]
        suffix = _dot01_right(lft, u_ref[...])
        kt = k_refs[g][...].reshape(width, page).astype(BF16)
        scores.append(jnp.dot(qbd, kt, preferred_element_type=F32) + (suffix + carry))
        carry = carry + jnp.sum(lft, axis=-1, keepdims=True)
    carry_scr[...] = carry - lf_new

    mx = scores[0]
    for s in scores[1:]:
        mx = jnp.maximum(mx, s)
    m_old = m_scr[...]
    m_new = jnp.maximum(m_old, jnp.max(mx, axis=-1, keepdims=True))
    alpha = jnp.exp(m_old - m_new)
    psum = jnp.zeros(m_old.shape, F32)
    pv = jnp.zeros(acc_scr.shape, F32)
    for s, g in zip(scores, reversed(range(n_pages))):
        p = jnp.exp(s - m_new)
        psum = psum + p
        vt = v_refs[g][...].reshape(width, page).astype(BF16)
        pv = pv + lax.dot_general(p.astype(BF16), vt, NT_DIMS, preferred_element_type=F32)
    l_scr[...] = alpha * l_scr[...] + jnp.sum(psum, axis=-1, keepdims=True)
    acc_scr[...] = alpha[:, 0:1] * acc_scr[...] + pv
    m_scr[...] = m_new

    @pl.when(j == pl.num_programs(1) - 1)
    def _():
        o = jnp.where(head_lane, acc_scr[...] / l_scr[:, 0:1], 0.0)
        o_ref[...] = jnp.sum(o, axis=0, keepdims=True)


def _decode(layer, page_table, qbd, q, knew, vnew, lfnew, cache_kt, cache_vt, cache_lft, n_pages):
    db, total_pages = page_table.shape
    page = cache_kt.shape[-1]
    groups = total_pages // n_pages
    strict_upper = (lax.broadcasted_iota(jnp.int32, (page, page), 0)
                    > lax.broadcasted_iota(jnp.int32, (page, page), 1)).astype(BF16)

    def page_map(nd):
        def make(g):
            return lambda b, j, pt: (layer, pt[b, (groups - 1 - j) * n_pages + g]) + (0,) * nd
        return make

    seq3 = lambda b, j, pt: (b, 0, 0)
    kv_specs = [pl.BlockSpec((None, None, FOX_HEADS, FOX_DH, page), page_map(3)(g)) for g in range(n_pages)]
    lf_specs = [pl.BlockSpec((None, None, FOX_HEADS, page), page_map(2)(g)) for g in range(n_pages)]
    head_spec = pl.BlockSpec((None, FOX_HEADS, FOX_DH), seq3)
    flat_spec = pl.BlockSpec((None, 1, D_FOX), seq3)
    grid_spec = pltpu.PrefetchScalarGridSpec(
        num_scalar_prefetch=1,
        grid=(db, groups),
        in_specs=[pl.BlockSpec((None, FOX_HEADS, D_FOX), seq3), head_spec, head_spec, flat_spec,
                  pl.BlockSpec((None, FOX_HEADS, 1), seq3), pl.BlockSpec((page, page), lambda b, j, pt: (0, 0))]
                 + kv_specs + kv_specs + lf_specs,
        out_specs=flat_spec,
        scratch_shapes=[pltpu.VMEM((FOX_HEADS, LANES), F32), pltpu.VMEM((FOX_HEADS, LANES), F32),
                        pltpu.VMEM((FOX_HEADS, D_FOX), F32), pltpu.VMEM((FOX_HEADS, LANES), F32)],
    )
    return pl.pallas_call(
        functools.partial(_decode_body, n_pages=n_pages),
        grid_spec=grid_spec,
        out_shape=jax.ShapeDtypeStruct((db, 1, D_FOX), F32),
        compiler_params=_cparams(("parallel", "arbitrary")),
        name="fox_decode",
    )(page_table, qbd, q, knew, vnew, lfnew, strict_upper,
      *([cache_kt] * n_pages), *([cache_vt] * n_pages), *([cache_lft] * n_pages))


def _merge_ffn_body(h_ref, og_ref, gg_ref, of_ref, gn_ref, wo_ref, nw_ref, wup_ref, wdn_ref, fn_ref, o_ref,
                    *, final):
    og = og_ref[...]
    gg = gg_ref[...]
    gn = gn_ref[...]
    parts = []
    for hd in range(GLA_HEADS):
        oh = og[:, hd * GLA_DV:(hd + 1) * GLA_DV]
        on = oh * lax.rsqrt(jnp.mean(oh * oh, axis=-1, keepdims=True) + RMS_EPS) * gn
        gh = gg[:, hd * GLA_DV:(hd + 1) * GLA_DV]
        parts.append((on * (gh * jax.nn.sigmoid(gh))).astype(BF16))
    gla_part = jnp.concatenate(parts, axis=-1)
    mixed = (jnp.dot(gla_part, wo_ref[:D_GLA, :], preferred_element_type=F32)
             + jnp.dot(of_ref[...], wo_ref[D_GLA:, :], preferred_element_type=F32))
    x = h_ref[...] + mixed

    out = _half_ffn(x, nw_ref[...], wup_ref, wdn_ref)
    o_ref[...] = _rms(out, fn_ref[...]) if final else out


def _merge_ffn(h, o_gla, gg, o_fox, gn, wo, nw, wup, wdn, fn, tm, final):
    rows, d = h.shape
    row_spec = lambda w: pl.BlockSpec((tm, w), lambda i: (i, 0))
    return pl.pallas_call(
        functools.partial(_merge_ffn_body, final=final),
        grid=(rows // tm,),
        in_specs=[row_spec(d), row_spec(D_GLA), row_spec(D_GLA), row_spec(D_FOX),
                  _resident(gn.shape), _resident(wo.shape), _resident((1, d)), _resident(wup.shape),
                  _resident(wdn.shape), _resident((1, d))],
        out_specs=row_spec(d),
        out_shape=jax.ShapeDtypeStruct((rows, d), F32),
        compiler_params=_cparams(("parallel",)),
        name="merge_ffn",
    )(h, o_gla, gg, o_fox, gn, wo, nw, wup, wdn, fn)


def _pick(n, prefs):
    for t in prefs:
        if n % t == 0:
            return t
    raise ValueError(f"no tile in {prefs} divides {n}")


def _pack_w_in(w_in_l, b_f_l, w_a2_l, b_a2_l):
    sizes = (GLA_QK, GLA_QK, D_GLA, D_GLA, GLA_RANK, D_FOX, D_FOX, D_FOX, FOX_HEADS)
    offs = [0]
    for s in sizes:
        offs.append(offs[-1] + s)
    col = lambda i: w_in_l[:, offs[i]:offs[i + 1]]
    wmain = jnp.concatenate([col(0), col(1), col(2), col(3), col(5), col(6), col(7)], axis=1).astype(BF16)
    d = w_in_l.shape[0]
    pad = LANES - FOX_HEADS - GLA_RANK
    wsmall = jnp.concatenate([col(8), col(4), jnp.zeros((d, pad), F32)], axis=1).astype(BF16)
    bsm = jnp.concatenate([b_f_l, jnp.zeros((LANES - FOX_HEADS,), F32)])[None, :]
    wa2 = jnp.concatenate([jnp.zeros((FOX_HEADS, GLA_QK), F32), w_a2_l, jnp.zeros((pad, GLA_QK), F32)],
                          axis=0).astype(BF16)
    return wmain, wsmall, wa2, b_a2_l[None, :], bsm


def kernel(x_prompt, x_sample, cache_fox_k, cache_fox_v, cache_fox_logf, state_gla, page_table, meta_tokens,
           ffn1_norm, ffn1_w_up, ffn1_w_down, mix_norm, w_in, fox_b_f, gla_w_a2, gla_b_a2, gla_norm, w_out,
           ffn2_norm, ffn2_w_up, ffn2_w_down, final_norm):
    b, seq, d = x_prompt.shape
    db, dec_seq, _ = x_sample.shape
    n_meta = meta_tokens.shape[0]
    depth = w_in.shape[0]
    n_pool, page = cache_fox_k.shape[1], cache_fox_k.shape[2]
    assert dec_seq == 1 and n_meta + db <= SIDE_ROWS and page == LANES
    s0, s1 = n_meta, n_meta + db

    tm = _pick(seq, (512, 256, 128))
    tq = _pick(seq, (1024, 512, 256, 128))
    cs = _pick(seq, (64,))
    pages_per_step = _pick(page_table.shape[1], (32, 16, 8, 4, 2, 1))

    hm = x_prompt.reshape(b * seq, d)
    side = jnp.concatenate([meta_tokens, x_sample[:, 0, :], jnp.zeros((SIDE_ROWS - s1, d), F32)], axis=0)

    cache_kt = jnp.transpose(cache_fox_k, (0, 1, 3, 4, 2))
    cache_vt = jnp.transpose(cache_fox_v, (0, 1, 3, 4, 2))
    cache_lft = jnp.swapaxes(cache_fox_logf, -1, -2)

    outs = {n: [] for n in ("pk", "pv", "plf", "ps", "sk", "sv", "slf", "ss")}
    for l in range(depth):
        up1, dn1 = ffn1_w_up[l].astype(BF16), ffn1_w_down[l].astype(BF16)
        up2, dn2 = ffn2_w_up[l].astype(BF16), ffn2_w_down[l].astype(BF16)
        wmain, wsmall, wa2, ba2, bsm = _pack_w_in(w_in[l], fox_b_f[l], gla_w_a2[l], gla_b_a2[l])
        wo = w_out[l].astype(BF16)

        n1, nm = ffn1_norm[l][None, :], mix_norm[l][None, :]
        pm = _ffn_proj(hm, n1, up1, dn1, nm, wmain, wsmall, wa2, ba2, bsm, tm, seq)
        sd = _ffn_proj(side, n1, up1, dn1, nm, wmain, wsmall, wa2, ba2, bsm, SIDE_ROWS, SIDE_ROWS,
                       prefix_rows=n_meta)
        hm, side = pm["h"], sd["h"]

        zero_state = jnp.zeros((1, 2, 2 * GLA_DV, LANES), F32)
        o_gla_meta, st_meta = _gla(sd["gq"][None], sd["gk"][None], sd["gv"][None], sd["glog"][None],
                                   zero_state, SIDE_ROWS, 1, n_valid=n_meta)
        r3 = lambda a: a.reshape(b, seq, a.shape[-1])
        o_gla, st_prompt = _gla(r3(pm["gq"]), r3(pm["gk"]), r3(pm["gv"]), r3(pm["glog"]), st_meta, cs, b,
                                n_sub=2 if seq % (2 * cs) == 0 else 1)
        pad_tok = lambda a: jnp.pad(a[s0:s1][:, None, :], ((0, 0), (0, SIDE_ROWS - 1), (0, 0)))
        o_gla_s, st_sample = _gla(pad_tok(sd["gq"]), pad_tok(sd["gk"]), pad_tok(sd["gv"]), pad_tok(sd["glog"]),
                                  _state_to_pairs(state_gla[l]), SIDE_ROWS, _pick(db, (4, 2, 1)), n_valid=1)

        o_fox_meta = _fox(sd["qa"][None], sd["ka"][None], sd["va"][None], SIDE_ROWS)
        o_fox = _fox(r3(pm["qa"]), r3(pm["ka"]), r3(pm["va"]), tq, prefix=(sd["ka"], sd["va"]))

        hd3 = lambda a: a[s0:s1].reshape(db, FOX_HEADS, FOX_DH)
        q_s = sd["qa"][s0:s1].reshape(db, FOX_HEADS, LANES)[:, :, :FOX_DH]
        head_of_lane = jnp.arange(D_FOX)[None, None, :] // FOX_DH
        qbd = jnp.where(head_of_lane == jnp.arange(FOX_HEADS)[None, :, None], q_s.reshape(db, 1, D_FOX), 0)
        o_fox_s = _decode(l, page_table, qbd.astype(BF16), q_s.astype(F32), hd3(sd["fk"]),
                          sd["fv"][s0:s1][:, None, :], sd["lf"][s0:s1][:, :, None],
                          cache_kt, cache_vt, cache_lft, pages_per_step)

        tail = jnp.zeros((SIDE_ROWS - s1, D_GLA), F32)
        og_side = jnp.concatenate([o_gla_meta[0, :n_meta], o_gla_s[:, 0, :], tail], axis=0)
        of_side = jnp.concatenate([o_fox_meta[0, :n_meta], o_fox_s[:, 0, :].astype(BF16), tail.astype(BF16)], axis=0)
        gn = gla_norm[l][None, :]
        last = l == depth - 1
        fn = final_norm[None, :]
        n2 = ffn2_norm[l][None, :]
        hm = _merge_ffn(hm, o_gla.reshape(b * seq, D_GLA), pm["gg"], o_fox.reshape(b * seq, D_FOX), gn, wo,
                        n2, up2, dn2, fn, tm, last)
        side = _merge_ffn(side, og_side, sd["gg"], of_side, gn, wo, n2, up2, dn2, fn, SIDE_ROWS, last)

        heads = lambda a: a.reshape(a.shape[:-1] + (FOX_HEADS, FOX_DH))
        with_meta = lambda m, r: jnp.concatenate(
            [jnp.broadcast_to(m[None, :n_meta], (b, n_meta) + m.shape[1:]), r.reshape((b, seq) + r.shape[1:])], axis=1)
        outs["pk"].append(heads(with_meta(sd["fk"], pm["fk"])))
        outs["pv"].append(heads(with_meta(sd["fv"], pm["fv"])))
        outs["plf"].append(with_meta(sd["lf"], pm["lf"]))
        outs["ps"].append(_pairs_to_state(st_prompt))
        outs["sk"].append(heads(sd["fk"][s0:s1][:, None, :]))
        outs["sv"].append(heads(sd["fv"][s0:s1][:, None, :]))
        outs["slf"].append(sd["lf"][s0:s1][:, None, :])
        outs["ss"].append(_pairs_to_state(st_sample))

    y_prompt = hm.reshape(b, seq, d)
    y_sample = side[s0:s1][:, None, :]
    st = lambda n: jnp.stack(outs[n])
    return (y_prompt, y_sample, st("pk"), st("pv"), st("plf"), st("ps"), st("sk"), st("sv"), st("slf"), st("ss"))
```
